```python
import jax, jax.numpy as jnp
from jax import lax
import numpy as np

D_MODEL = 2048
BATCH = 8
SEQ = 2048
DEPTH = 2
DEC_BATCH = 128
DEC_SEQ = 4
PAST_LEN = 2048
PAGE_SIZE = 128

N_A = DEPTH // 2
N_B = DEPTH - N_A
POOL_EXPAND = 2
W_A = POOL_EXPAND * D_MODEL
POOL_WINDOWS = (2, 4, 8, 16)
N_POOL_GROUPS = len(POOL_WINDOWS)
G_A = W_A // N_POOL_GROUPS
POOL_STATE = max(POOL_WINDOWS) - 1
N_HEADS = 16
HEAD_DIM = 128
W_B = N_HEADS * HEAD_DIM
Q_BLOCK = 128
SB_BIAS_INIT = -7.0
PLE_DIM = 256
EPS = 1e-6

kernel_name = "yoco_pool_stickbreak_decode_step"


def rms_norm(x, g):
    xf = x.astype(jnp.float32)
    y = xf * lax.rsqrt(jnp.mean(xf * xf, axis=-1, keepdims=True) + EPS)
    return (y * g.astype(jnp.float32)).astype(x.dtype)


def multiscale_pool(u, prev):
    ext = u if prev is None else jnp.concatenate([prev.astype(u.dtype), u], axis=1)
    T = u.shape[1]
    n_prev = ext.shape[1] - T
    ef = ext.astype(jnp.float32)
    csum = jnp.concatenate([jnp.zeros_like(ef[:, :1]), jnp.cumsum(ef, axis=1)], axis=1)
    idx = n_prev + np.arange(T)
    outs = []
    for gi, w in enumerate(POOL_WINDOWS):
        sl = slice(gi * G_A, (gi + 1) * G_A)
        lo = np.maximum(idx + 1 - w, 0)
        cnt = (idx + 1 - lo).astype(np.float32)
        c = csum[:, :, sl]
        mean = (c[:, idx + 1] - c[:, lo]) / cnt[None, :, None]
        outs.append(mean - ef[:, idx, sl])
    return jnp.stack(outs, axis=2), ext[:, -POOL_STATE:]


def pool_mixer(h, prev, w_in, w_group, scale, w_out):
    ug = h @ w_in
    u, gate = ug[..., :W_A], ug[..., W_A:]
    pooled, new_state = multiscale_pool(u, prev)
    mixed = jnp.einsum('btng,ngh->btnh', pooled.astype(h.dtype), w_group).reshape(u.shape)
    y = mixed * scale * jax.nn.silu(gate)
    return y @ w_out, new_state


def stick_breaking(q, k_new, v_new, k_past, v_past, bias):
    pos0 = 0 if k_past is None else k_past.shape[1]
    Tq = q.shape[1]
    scale = HEAD_DIM ** -0.5
    bias_f = bias.astype(jnp.float32)[None, :, None, None]
    outs = []
    for b0 in range(0, Tq, Q_BLOCK):
        b1 = min(b0 + Q_BLOCK, Tq)
        qb = q[:, b0:b1]
        kb, vb = k_new[:, :b1], v_new[:, :b1]
        z = jnp.einsum('bqhd,bkhd->bhqk', qb, kb)
        if k_past is not None:
            z = jnp.concatenate([jnp.einsum('bqhd,bkhd->bhqk', qb, k_past), z], axis=-1)
        z = z.astype(jnp.float32) * scale + bias_f
        q_pos = pos0 + np.arange(b0, b1)
        k_pos = np.arange(pos0 + b1)
        mask = k_pos[None, :] < q_pos[:, None]
        log_beta = jax.nn.log_sigmoid(z)
        log_keep = jnp.where(mask, jax.nn.log_sigmoid(-z), 0.0)
        later = lax.cumsum(log_keep, axis=3, reverse=True) - log_keep
        a = jnp.where(mask, jnp.exp(log_beta + later), 0.0).astype(v_new.dtype)
        o = jnp.einsum('bhqk,bkhd->bqhd', a[..., pos0:], vb)
        if v_past is not None:
            o = o + jnp.einsum('bhqk,bkhd->bqhd', a[..., :pos0], v_past)
        outs.append(o)
    return jnp.concatenate(outs, axis=1)


def sb_mixer(h, k_new, v_new, k_past, v_past, w_in, bias, w_out):
    B, T, _ = h.shape
    qg = h @ w_in
    q = qg[..., :W_B].reshape(B, T, N_HEADS, HEAD_DIM)
    gate = qg[..., W_B:]
    o = stick_breaking(q, k_new, v_new, k_past, v_past, bias).reshape(B, T, W_B)
    return (o * jax.nn.silu(gate)) @ w_out


def trunk(x, p, pool_prev, k_past, v_past, pre_norm_g, post_norm_g, a_w_in, a_w_group,
          a_scale, a_w_out, kv_norm_g, w_kv, b_w_in, b_logit_bias, b_w_out, ple_norm_g,
          ple_w_gate, ple_w_proj):
    B, T, _ = x.shape
    pool_states = []
    k_new = v_new = None
    for i in range(DEPTH):
        h = rms_norm(x, pre_norm_g[i])
        if i < N_A:
            prev = None if pool_prev is None else pool_prev[i]
            y, st = pool_mixer(h, prev, a_w_in[i], a_w_group[i], a_scale[i], a_w_out[i])
            pool_states.append(st)
        else:
            j = i - N_A
            y = sb_mixer(h, k_new, v_new, k_past, v_past, b_w_in[j], b_logit_bias[j], b_w_out[j])
        x = x + rms_norm(y, post_norm_g[i])
        gate = jax.nn.sigmoid((rms_norm(x, ple_norm_g[i]) @ ple_w_gate[i]).astype(jnp.float32))
        x = x + (p[i] @ ple_w_proj[i]) * gate.astype(x.dtype)
        if i == N_A - 1:
            kv = rms_norm(x, kv_norm_g) @ w_kv
            k_new = kv[..., :W_B].reshape(B, T, N_HEADS, HEAD_DIM)
            v_new = kv[..., W_B:].reshape(B, T, N_HEADS, HEAD_DIM)
    return x, k_new, v_new, jnp.stack(pool_states, axis=0)


def setup_inputs(seed: int = 0) -> dict:
    key = jax.random.key(seed)
    ks = jax.random.split(key, 24)
    n_pages = PAST_LEN // PAGE_SIZE
    n_used = DEC_BATCH * n_pages
    n_phys = n_used + max(1, n_used // 4)
    f32 = jnp.float32
    nrm = lambda k, shape, s=1.0: jax.random.normal(k, shape, f32) * s
    page_table = jax.random.permutation(ks[0], n_phys)[:n_used].reshape(DEC_BATCH, n_pages).astype(jnp.int32)
    return {
        "x_prompt": nrm(ks[1], (BATCH, SEQ, D_MODEL)),
        "x_sample": nrm(ks[2], (DEC_BATCH, DEC_SEQ, D_MODEL)),
        "cache_k": nrm(ks[3], (n_phys, PAGE_SIZE, N_HEADS, HEAD_DIM)),
        "cache_v": nrm(ks[4], (n_phys, PAGE_SIZE, N_HEADS, HEAD_DIM)),
        "state_pool": nrm(ks[5], (N_A, DEC_BATCH, POOL_STATE, W_A)),
        "page_table": page_table,
        "p_prompt": nrm(ks[6], (DEPTH, BATCH, SEQ, PLE_DIM)),
        "p_sample": nrm(ks[7], (DEPTH, DEC_BATCH, DEC_SEQ, PLE_DIM)),
        "pre_norm_g": 1.0 + nrm(ks[8], (DEPTH, D_MODEL), 0.05),
        "post_norm_g": 1.0 + nrm(ks[9], (DEPTH, D_MODEL), 0.05),
        "a_w_in": nrm(ks[10], (N_A, D_MODEL, 2 * W_A), D_MODEL ** -0.5),
        "a_w_group": nrm(ks[11], (N_A, N_POOL_GROUPS, G_A, G_A), G_A ** -0.5),
        "a_scale": 1.0 + nrm(ks[12], (N_A, W_A), 0.1),
        "a_w_out": nrm(ks[13], (N_A, W_A, D_MODEL), W_A ** -0.5),
        "kv_norm_g": 1.0 + nrm(ks[14], (D_MODEL,), 0.05),
        "w_kv": nrm(ks[15], (D_MODEL, 2 * W_B), D_MODEL ** -0.5),
        "b_w_in": nrm(ks[16], (N_B, D_MODEL, 2 * W_B), D_MODEL ** -0.5),
        "b_logit_bias": SB_BIAS_INIT + nrm(ks[21], (N_B, N_HEADS), 0.1),
        "b_w_out": nrm(ks[17], (N_B, W_B, D_MODEL), W_B ** -0.5),
        "ple_norm_g": 1.0 + nrm(ks[18], (DEPTH, D_MODEL), 0.05),
        "ple_w_gate": nrm(ks[19], (DEPTH, D_MODEL, D_MODEL), D_MODEL ** -0.5),
        "ple_w_proj": nrm(ks[20], (DEPTH, PLE_DIM, D_MODEL), PLE_DIM ** -0.5),
    }


def reference(x_prompt, x_sample, cache_k, cache_v, state_pool, page_table, p_prompt, p_sample,
              pre_norm_g, post_norm_g, a_w_in, a_w_group, a_scale, a_w_out, kv_norm_g, w_kv,
              b_w_in, b_logit_bias, b_w_out, ple_norm_g, ple_w_gate, ple_w_proj):
    params = (pre_norm_g, post_norm_g, a_w_in, a_w_group, a_scale, a_w_out, kv_norm_g, w_kv,
              b_w_in, b_logit_bias, b_w_out, ple_norm_g, ple_w_gate, ple_w_proj)
    y_prompt, k_rows_prompt, v_rows_prompt, pool_state_prompt = trunk(
        x_prompt, p_prompt, None, None, None, *params)
    n_seq, n_pages = page_table.shape
    past = n_pages * PAGE_SIZE
    k_past = cache_k[page_table].reshape(n_seq, past, N_HEADS, HEAD_DIM)
    v_past = cache_v[page_table].reshape(n_seq, past, N_HEADS, HEAD_DIM)
    y_sample, k_rows_sample, v_rows_sample, pool_state_sample = trunk(
        x_sample, p_sample, state_pool, k_past, v_past, *params)
    return (y_prompt, y_sample, k_rows_prompt, v_rows_prompt, k_rows_sample, v_rows_sample,
            pool_state_prompt, pool_state_sample)
```

```python
import functools

import jax
import jax.numpy as jnp
from jax import lax
from jax.experimental import pallas as pl
from jax.experimental.pallas import tpu as pltpu

F32 = jnp.float32
BF16 = jnp.bfloat16

EPS = 1e-6
POOL_WINDOWS = (2, 4, 8, 16)
POOL_CARRY = 16
HEAD_DIM = 128
KEY_BLOCK = 128
Q_PAD = 16
VMEM_LIMIT = 56 * 1024 * 1024


def _params(sem):
    return pltpu.CompilerParams(dimension_semantics=sem, vmem_limit_bytes=VMEM_LIMIT)


def _rms_unit(x):
    return x * lax.rsqrt(jnp.mean(x * x, axis=-1, keepdims=True) + EPS)


def _silu(x):
    return x * jax.nn.sigmoid(x)


def _resident(shape):
    nd = len(shape)
    return pl.BlockSpec(shape, lambda *_: (0,) * nd, pipeline_mode=pl.Buffered(1))


def _inproj_kernel(x_ref, g_ref, w_ref, u_ref, sg_ref, *rest, nu, tm, tn, tiles_per_seq,
                   group_width, fuse_pool):
    if fuse_pool:
        st_ref, h_ref, ext_ref, carry_ref = rest
    else:
        (h_ref,) = rest
    i = pl.program_id(0)
    j = pl.program_id(1)

    @pl.when(j == 0)
    def _():
        h_ref[...] = (_rms_unit(x_ref[...]) * g_ref[...]).astype(BF16)

    acc = jnp.dot(h_ref[...], w_ref[...], preferred_element_type=F32)

    @pl.when(j < nu)
    def _():
        if not fuse_pool:
            u_ref[...] = acc
            return
        seq_tile = i % tiles_per_seq
        tail = acc[tm - POOL_CARRY:, :]
        prev = jnp.where(seq_tile == 0, 0.0, carry_ref[j])
        ext_ref[0:POOL_CARRY, :] = prev
        ext_ref[POOL_CARRY:, :] = acc
        carry_ref[j] = tail
        st_ref[0] = tail
        pos = lax.broadcasted_iota(jnp.int32, (tm, 1), 0) + seq_tile * tm
        group = (j * tn) // group_width
        for gi, w in enumerate(POOL_WINDOWS):
            @pl.when(group == gi)
            def _(w=w):
                s = ext_ref[POOL_CARRY:POOL_CARRY + tm, :]
                for k in range(1, w):
                    s = s + ext_ref[POOL_CARRY - k:POOL_CARRY - k + tm, :]
                cnt = jnp.minimum(pos + 1, w).astype(F32)
                u_ref[...] = (s / cnt - acc).astype(u_ref.dtype)

    @pl.when(j >= nu)
    def _():
        sg_ref[...] = _silu(acc).astype(sg_ref.dtype)


def _inproj(x, g, w, *, seq_len, fuse_pool):
    n, d = x.shape
    w2 = w.shape[1]
    wa = w2 // 2
    group_width = wa // len(POOL_WINDOWS)
    tn = min(1024, group_width)
    tm = min(1024, seq_len if fuse_pool else n)
    assert n % tm == 0 and wa % tn == 0 and group_width % tn == 0
    nu = wa // tn
    grid = (n // tm, 2 * nu)
    u_dtype = BF16 if fuse_pool else F32
    out_shape = [jax.ShapeDtypeStruct((n, wa), u_dtype), jax.ShapeDtypeStruct((n, wa), BF16)]
    out_specs = [pl.BlockSpec((tm, tn), lambda i, j: (i, jnp.minimum(j, nu - 1))),
                 pl.BlockSpec((tm, tn), lambda i, j: (i, jnp.maximum(j - nu, 0)))]
    scratch = [pltpu.VMEM((tm, d), BF16)]
    tiles_per_seq = 1
    if fuse_pool:
        assert seq_len % tm == 0 and tm >= POOL_CARRY
        tiles_per_seq = seq_len // tm
        out_shape.append(jax.ShapeDtypeStruct((n // seq_len, POOL_CARRY, wa), F32))
        out_specs.append(pl.BlockSpec((1, POOL_CARRY, tn),
                                      lambda i, j: (i // tiles_per_seq, 0, jnp.minimum(j, nu - 1))))
        scratch += [pltpu.VMEM((POOL_CARRY + tm, tn), F32), pltpu.VMEM((nu, POOL_CARRY, tn), F32)]
    kern = functools.partial(_inproj_kernel, nu=nu, tm=tm, tn=tn, tiles_per_seq=tiles_per_seq,
                             group_width=group_width, fuse_pool=fuse_pool)
    return pl.pallas_call(
        kern, grid=grid,
        in_specs=[pl.BlockSpec((tm, d), lambda i, j: (i, 0)),
                  pl.BlockSpec((1, d), lambda i, j: (0, 0)),
                  pl.BlockSpec((d, tn), lambda i, j: (0, j))],
        out_specs=out_specs, out_shape=out_shape, scratch_shapes=scratch,
        compiler_params=_params(("arbitrary", "arbitrary")), name="inproj_pool",
    )(x, g, w)


def _pool_sample_kernel(ext_ref, o_ref, *, n_prev, t_new):
    group = pl.program_id(1)
    for gi, w in enumerate(POOL_WINDOWS):
        @pl.when(group == gi)
        def _(w=w):
            cur = ext_ref[:, n_prev:n_prev + t_new, :]
            s = cur
            for k in range(1, w):
                s = s + ext_ref[:, n_prev - k:n_prev - k + t_new, :]
            o_ref[...] = (s / float(w) - cur).astype(o_ref.dtype)


def _pool_sample(ext, t_new):
    s, rows, wa = ext.shape
    n_prev = rows - t_new
    assert n_prev >= max(POOL_WINDOWS) - 1
    gw = wa // len(POOL_WINDOWS)
    bs = min(16, s)
    assert s % bs == 0
    return pl.pallas_call(
        functools.partial(_pool_sample_kernel, n_prev=n_prev, t_new=t_new),
        grid=(s // bs, len(POOL_WINDOWS)),
        in_specs=[pl.BlockSpec((bs, rows, gw), lambda i, g: (i, 0, g))],
        out_specs=pl.BlockSpec((bs, t_new, gw), lambda i, g: (i, 0, g)),
        out_shape=jax.ShapeDtypeStruct((s, t_new, wa), BF16),
        compiler_params=_params(("arbitrary", "arbitrary")), name="pool_sample",
    )(ext)


def _group_kernel(p_ref, w_ref, scale_ref, sg_ref, y_ref):
    mixed = jnp.dot(p_ref[...], w_ref[0], preferred_element_type=F32)
    y_ref[...] = (mixed * scale_ref[...] * sg_ref[...].astype(F32)).astype(y_ref.dtype)


def _group_mix(pooled, w_group, scale, sgate):
    n, wa = pooled.shape
    ng, gw, _ = w_group.shape
    tm = min(1024, n)
    assert n % tm == 0
    return pl.pallas_call(
        _group_kernel, grid=(n // tm, ng),
        in_specs=[pl.BlockSpec((tm, gw), lambda i, g: (i, g)),
                  pl.BlockSpec((1, gw, gw), lambda i, g: (g, 0, 0)),
                  pl.BlockSpec((1, gw), lambda i, g: (0, g)),
                  pl.BlockSpec((tm, gw), lambda i, g: (i, g))],
        out_specs=pl.BlockSpec((tm, gw), lambda i, g: (i, g)),
        out_shape=jax.ShapeDtypeStruct((n, wa), BF16),
        compiler_params=_params(("arbitrary", "arbitrary")), name="group_mix",
    )(pooled, w_group, scale, sgate)


def _outproj_kernel(*refs, gated):
    if gated:
        y_ref, sg_ref, w_ref, x_ref, gpost_ref, gple_ref, x1_ref, hp_ref = refs
        y = (y_ref[...] * sg_ref[...].astype(F32)).astype(BF16)
    else:
        y_ref, w_ref, x_ref, gpost_ref, gple_ref, x1_ref, hp_ref = refs
        y = y_ref[...]
    out = jnp.dot(y, w_ref[...], preferred_element_type=F32)
    x1 = x_ref[...] + _rms_unit(out) * gpost_ref[...]
    x1_ref[...] = x1
    hp_ref[...] = (_rms_unit(x1) * gple_ref[...]).astype(BF16)


def _outproj(y, sgate, w_out, x, g_post, g_ple):
    n, k = y.shape
    d = x.shape[1]
    tm = min(512, n)
    assert n % tm == 0
    gated = sgate is not None
    row = lambda width: pl.BlockSpec((tm, width), lambda i: (i, 0))
    in_specs = [row(k)] + ([row(k)] if gated else []) + [
        _resident((k, d)), row(d), _resident((1, d)), _resident((1, d))]
    args = [y] + ([sgate] if gated else []) + [w_out, x, g_post, g_ple]
    return pl.pallas_call(
        functools.partial(_outproj_kernel, gated=gated), grid=(n // tm,),
        in_specs=in_specs, out_specs=[row(d), row(d)],
        out_shape=[jax.ShapeDtypeStruct((n, d), F32), jax.ShapeDtypeStruct((n, d), BF16)],
        compiler_params=_params(("arbitrary",)), name="outproj_post",
    )(*args)


def _ple_kernel(*refs, n_norm):
    hp_ref, wg_ref, p_ref, wp_ref, x1_ref = refs[:5]
    g_refs = refs[5:5 + n_norm]
    x2_ref = refs[5 + n_norm]
    h_refs = refs[6 + n_norm:]
    gate = jax.nn.sigmoid(jnp.dot(hp_ref[...], wg_ref[...], preferred_element_type=F32))
    proj = jnp.dot(p_ref[...].astype(BF16), wp_ref[...], preferred_element_type=F32)
    x2 = x1_ref[...] + proj * gate
    x2_ref[...] = x2
    if n_norm:
        xn = _rms_unit(x2)
        for g_ref, h_ref in zip(g_refs, h_refs):
            h_ref[...] = (xn * g_ref[...]).astype(BF16)


def _ple(hp, w_gate, p, w_proj, x1, norm_gains):
    n, d = hp.shape
    pd = p.shape[1]
    tm = min(512, n)
    assert n % tm == 0
    n_norm = len(norm_gains)
    row = lambda width: pl.BlockSpec((tm, width), lambda i: (i, 0))
    in_specs = [row(d), _resident((d, d)), row(pd), _resident((pd, d)), row(d)] + \
               [_resident((1, d))] * n_norm
    out = pl.pallas_call(
        functools.partial(_ple_kernel, n_norm=n_norm), grid=(n // tm,),
        in_specs=in_specs, out_specs=[row(d)] * (1 + n_norm),
        out_shape=[jax.ShapeDtypeStruct((n, d), F32)] +
                  [jax.ShapeDtypeStruct((n, d), BF16)] * n_norm,
        compiler_params=_params(("arbitrary",)), name="ple",
    )(hp, w_gate, p, w_proj, x1, *norm_gains)
    return out[0], out[1:]


def _matmul_split_kernel(x_ref, w_ref, o0_ref, o1_ref, *, ns, f0, f1):
    j = pl.program_id(1)
    acc = jnp.dot(x_ref[...], w_ref[...], preferred_element_type=F32)

    @pl.when(j < ns)
    def _():
        o0_ref[...] = f0(acc).astype(o0_ref.dtype)

    @pl.when(j >= ns)
    def _():
        o1_ref[...] = f1(acc).astype(o1_ref.dtype)


def _matmul_split(x, w, dtypes, f0, f1, name):
    n, k = x.shape
    half = w.shape[1] // 2
    tm = min(1024, n)
    tn = min(1024, half)
    assert n % tm == 0 and half % tn == 0
    ns = half // tn
    return pl.pallas_call(
        functools.partial(_matmul_split_kernel, ns=ns, f0=f0, f1=f1), grid=(n // tm, 2 * ns),
        in_specs=[pl.BlockSpec((tm, k), lambda i, j: (i, 0)),
                  pl.BlockSpec((k, tn), lambda i, j: (0, j))],
        out_specs=[pl.BlockSpec((tm, tn), lambda i, j: (i, jnp.minimum(j, ns - 1))),
                   pl.BlockSpec((tm, tn), lambda i, j: (i, jnp.maximum(j - ns, 0)))],
        out_shape=[jax.ShapeDtypeStruct((n, half), dtypes[0]),
                   jax.ShapeDtypeStruct((n, half), dtypes[1])],
        compiler_params=_params(("arbitrary", "arbitrary")), name=name,
    )(x, w)


def _suffix_matrix():
    r = lax.broadcasted_iota(jnp.int32, (KEY_BLOCK, 2 * KEY_BLOCK), 0)
    c = lax.broadcasted_iota(jnp.int32, (KEY_BLOCK, 2 * KEY_BLOCK), 1)
    return jnp.where((c >= KEY_BLOCK) | (r > c), 1.0, 0.0).astype(BF16)


def _sb_weights(z, mask, run, suffix):
    sp = jnp.log1p(jnp.exp(-jnp.abs(z)))
    log_beta = jnp.minimum(z, 0.0) - sp
    neg_keep = jnp.maximum(z, 0.0) + sp
    if mask is not None:
        neg_keep = jnp.where(mask, neg_keep, 0.0)
    hi = neg_keep.astype(BF16)
    lo = (neg_keep - hi.astype(F32)).astype(BF16)
    cs = (jnp.dot(hi, suffix, preferred_element_type=F32) +
          jnp.dot(lo, suffix, preferred_element_type=F32))
    later = run + cs[:, :KEY_BLOCK]
    a = jnp.exp(log_beta - later)
    if mask is not None:
        a = jnp.where(mask, a, 0.0)
    return a, run + cs[:, KEY_BLOCK:]


def _sb_prompt_kernel(bias_ref, q_ref, k_ref, v_ref, sg_ref, o_ref, kb_ref, vb_ref, *, seq_len):
    h = pl.program_id(1)
    kb_ref[...] = k_ref[0].astype(BF16)
    vb_ref[...] = v_ref[0].astype(BF16)
    bias = bias_ref[h]
    scale = HEAD_DIM ** -0.5
    suffix = _suffix_matrix()
    diff = (lax.broadcasted_iota(jnp.int32, (KEY_BLOCK, KEY_BLOCK), 0) -
            lax.broadcasted_iota(jnp.int32, (KEY_BLOCK, KEY_BLOCK), 1))
    zeros = jnp.zeros((KEY_BLOCK, KEY_BLOCK), F32)

    def q_body(qi, _):
        rows = pl.ds(pl.multiple_of(qi * KEY_BLOCK, KEY_BLOCK), KEY_BLOCK)
        q = q_ref[0, rows, :]

        def k_body(t, carry):
            run, acc = carry
            kj = qi - t
            keys = pl.ds(pl.multiple_of(kj * KEY_BLOCK, KEY_BLOCK), KEY_BLOCK)
            s = lax.dot_general(q, kb_ref[keys, :], (((1,), (1,)), ((), ())),
                                preferred_element_type=F32)
            z = s * scale + bias
            mask = diff > (kj - qi) * KEY_BLOCK
            a, run = _sb_weights(z, mask, run, suffix)
            acc = acc + jnp.dot(a.astype(BF16), vb_ref[keys, :], preferred_element_type=F32)
            return run, acc

        _, acc = lax.fori_loop(0, qi + 1, k_body, (zeros, zeros))
        o_ref[0, rows, :] = (acc * sg_ref[0, rows, :].astype(F32)).astype(o_ref.dtype)
        return 0

    lax.fori_loop(0, seq_len // KEY_BLOCK, q_body, 0)


def _sb_prompt(q, k, v, sgate, bias, n_heads):
    b, t, wb = q.shape
    assert t % KEY_BLOCK == 0 and wb == n_heads * HEAD_DIM
    blk = pl.BlockSpec((1, t, HEAD_DIM), lambda bi, hi: (bi, 0, hi))
    return pl.pallas_call(
        functools.partial(_sb_prompt_kernel, seq_len=t), grid=(b, n_heads),
        in_specs=[pl.BlockSpec(memory_space=pltpu.SMEM), blk, blk, blk, blk],
        out_specs=blk,
        out_shape=jax.ShapeDtypeStruct((b, t, wb), BF16),
        scratch_shapes=[pltpu.VMEM((t, HEAD_DIM), BF16), pltpu.VMEM((t, HEAD_DIM), BF16)],
        compiler_params=_params(("arbitrary", "arbitrary")), name="sb_prompt",
    )(bias, q, k, v, sgate)


def _sb_sample_kernel(pt_ref, q_ref, bias_ref, kn_ref, vn_ref, kp_ref, vp_ref, o_ref,
                      knew_ref, vnew_ref, run_ref, acc_ref, *, n_heads, t_new):
    s = pl.program_id(0)
    p = pl.program_id(1)
    scale = HEAD_DIM ** -0.5
    suffix = _suffix_matrix()
    rows = n_heads * Q_PAD

    def block(kb, vb, mask):
        zs = []
        for h in range(n_heads):
            qh = q_ref[0, h * Q_PAD:(h + 1) * Q_PAD, :]
            kh = kb[:, h * HEAD_DIM:(h + 1) * HEAD_DIM]
            zs.append(lax.dot_general(qh, kh, (((1,), (1,)), ((), ())),
                                      preferred_element_type=F32))
        z = jnp.concatenate(zs, axis=0) * scale + bias_ref[...]
        a, run = _sb_weights(z, mask, run_ref[...], suffix)
        run_ref[...] = run
        a = a.astype(BF16)
        for h in range(n_heads):
            r = slice(h * Q_PAD, (h + 1) * Q_PAD)
            acc_ref[r, :] += jnp.dot(a[r, :], vb[:, h * HEAD_DIM:(h + 1) * HEAD_DIM],
                                     preferred_element_type=F32)

    @pl.when((s == 0) & (p == 0))
    def _():
        knew_ref[...] = jnp.zeros_like(knew_ref)
        vnew_ref[...] = jnp.zeros_like(vnew_ref)

    @pl.when(p == 0)
    def _():
        knew_ref[0:t_new, :] = kn_ref[0]
        vnew_ref[0:t_new, :] = vn_ref[0]
        run_ref[...] = jnp.zeros_like(run_ref)
        acc_ref[...] = jnp.zeros_like(acc_ref)
        qpos = lax.broadcasted_iota(jnp.int32, (rows, KEY_BLOCK), 0) % Q_PAD
        kpos = lax.broadcasted_iota(jnp.int32, (rows, KEY_BLOCK), 1)
        block(knew_ref[...].astype(BF16), vnew_ref[...].astype(BF16), kpos < qpos)

    block(kp_ref[0].astype(BF16), vp_ref[0].astype(BF16), None)

    @pl.when(p == pl.num_programs(1) - 1)
    def _():
        o_ref[0] = acc_ref[...]


def _sb_sample(q, bias_rows, k_new, v_new, cache_k, cache_v, page_table, n_heads):
    s, t_new, wb = k_new.shape
    n_pages = page_table.shape[1]
    assert cache_k.shape[1] == KEY_BLOCK and t_new < Q_PAD
    rows = n_heads * Q_PAD
    page = pl.BlockSpec((1, KEY_BLOCK, wb), lambda si, pi, pt: (pt[si, n_pages - 1 - pi], 0, 0))
    new = pl.BlockSpec((1, t_new, wb), lambda si, pi, pt: (si, 0, 0))
    qo = pl.BlockSpec((1, rows, HEAD_DIM), lambda si, pi, pt: (si, 0, 0))
    grid_spec = pltpu.PrefetchScalarGridSpec(
        num_scalar_prefetch=1, grid=(s, n_pages),
        in_specs=[qo, pl.BlockSpec((rows, KEY_BLOCK), lambda si, pi, pt: (0, 0)),
                  new, new, page, page],
        out_specs=qo,
        scratch_shapes=[pltpu.VMEM((KEY_BLOCK, wb), F32), pltpu.VMEM((KEY_BLOCK, wb), F32),
                        pltpu.VMEM((rows, KEY_BLOCK), F32), pltpu.VMEM((rows, HEAD_DIM), F32)])
    return pl.pallas_call(
        functools.partial(_sb_sample_kernel, n_heads=n_heads, t_new=t_new),
        grid_spec=grid_spec,
        out_shape=jax.ShapeDtypeStruct((s, rows, HEAD_DIM), F32),
        compiler_params=_params(("arbitrary", "arbitrary")), name="sb_sample",
    )(page_table, q, bias_rows, k_new, v_new, cache_k, cache_v)


def _trunk(x, p, weights, past):
    (pre_g, post_g, a_w_in, a_w_group, a_scale, a_w_out, kv_g, w_kv, b_w_in, b_bias, b_w_out,
     ple_g, ple_w_gate, ple_w_proj) = weights
    b, t, d = x.shape
    n = b * t
    n_heads = b_bias.shape[1]
    wa = a_w_out.shape[1]
    xf = x.reshape(n, d)
    pf = p.reshape(p.shape[0], n, p.shape[-1])
    gain = lambda g: g.reshape(1, -1)

    if past is None:
        pooled, sgate, tails = _inproj(xf, gain(pre_g[0]), a_w_in[0], seq_len=t, fuse_pool=True)
        pool_state = tails[:, 1:, :]
    else:
        u, sgate = _inproj(xf, gain(pre_g[0]), a_w_in[0], seq_len=t, fuse_pool=False)
        ext = jnp.concatenate([past["pool"], u.reshape(b, t, wa)], axis=1)
        pooled = _pool_sample(ext, t).reshape(n, wa)
        pool_state = ext[:, t:, :]
    y = _group_mix(pooled, a_w_group[0], gain(a_scale[0]), sgate)
    x1, hp = _outproj(y, None, a_w_out[0], xf, gain(post_g[0]), gain(ple_g[0]))
    x2, (hk, hb) = _ple(hp, ple_w_gate[0], pf[0], ple_w_proj[0], x1, [gain(kv_g), gain(pre_g[1])])

    ident = lambda a: a
    k_new, v_new = _matmul_split(hk, w_kv, (F32, F32), ident, ident, "kv_proj")
    q, sgate_b = _matmul_split(hb, b_w_in[0], (BF16, BF16), ident, _silu, "q_proj")

    if past is None:
        shp = (b, t, n_heads * HEAD_DIM)
        y2 = _sb_prompt(q.reshape(shp), k_new.reshape(shp), v_new.reshape(shp),
                        sgate_b.reshape(shp), b_bias[0], n_heads).reshape(n, -1)
        x3, hp = _outproj(y2, None, b_w_out[0], x2, gain(post_g[1]), gain(ple_g[1]))
    else:
        q4 = q.reshape(b, t, n_heads, HEAD_DIM).transpose(0, 2, 1, 3)
        q4 = jnp.pad(q4, ((0, 0), (0, 0), (0, Q_PAD - t), (0, 0)))
        bias_rows = jnp.broadcast_to(jnp.repeat(b_bias[0], Q_PAD)[:, None],
                                     (n_heads * Q_PAD, KEY_BLOCK))
        o = _sb_sample(q4.reshape(b, n_heads * Q_PAD, HEAD_DIM), bias_rows,
                       k_new.reshape(b, t, -1), v_new.reshape(b, t, -1),
                       past["cache_k"], past["cache_v"], past["page_table"], n_heads)
        o = o.reshape(b, n_heads, Q_PAD, HEAD_DIM)[:, :, :t].transpose(0, 2, 1, 3).reshape(n, -1)
        x3, hp = _outproj(o, sgate_b, b_w_out[0], x2, gain(post_g[1]), gain(ple_g[1]))
    x4, _ = _ple(hp, ple_w_gate[1], pf[1], ple_w_proj[1], x3, [])

    kv_shape = (b, t, n_heads, HEAD_DIM)
    return (x4.reshape(b, t, d), k_new.reshape(kv_shape), v_new.reshape(kv_shape),
            pool_state[None])


def kernel(x_prompt, x_sample, cache_k, cache_v, state_pool, page_table, p_prompt, p_sample,
           pre_norm_g, post_norm_g, a_w_in, a_w_group, a_scale, a_w_out, kv_norm_g, w_kv,
           b_w_in, b_logit_bias, b_w_out, ple_norm_g, ple_w_gate, ple_w_proj):
    assert pre_norm_g.shape[0] == 2 and a_w_in.shape[0] == 1 and b_w_in.shape[0] == 1
    bf = lambda a: a.astype(BF16)
    weights = (pre_norm_g, post_norm_g, bf(a_w_in), bf(a_w_group), a_scale, bf(a_w_out),
               kv_norm_g, bf(w_kv), bf(b_w_in), b_logit_bias, bf(b_w_out), ple_norm_g,
               bf(ple_w_gate), bf(ple_w_proj))
    n_phys, page = cache_k.shape[:2]
    past = {"pool": state_pool[0],
            "cache_k": cache_k.reshape(n_phys, page, -1),
            "cache_v": cache_v.reshape(n_phys, page, -1),
            "page_table": page_table}
    y_p, k_p, v_p, st_p = _trunk(x_prompt, p_prompt, weights, None)
    y_s, k_s, v_s, st_s = _trunk(x_sample, p_sample, weights, past)
    return (y_p, y_s, k_p, v_p, k_s, v_s, st_p, st_s)
```

```python
import functools

import jax
import jax.numpy as jnp
from jax import lax
from jax.experimental import pallas as pl
from jax.experimental.pallas import tpu as pltpu

F32 = jnp.float32
BF16 = jnp.bfloat16

EPS = 1e-6
POOL_WINDOWS = (2, 4, 8, 16)
POOL_CARRY = 16
LANES = 128
HEAD_DIM = 128
HEAD_OCTET = 8
KEY_BLOCK = 128
Q_PAD = 16
PROMPT_TQ = 512
PROMPT_KB = 256
PROMPT_DIAG_KB = 128
PROMPT_UNROLL = 2
SAMPLE_PAGES_PER_STEP = 2
LOG2E = 1.4426950408889634
VMEM_LIMIT = 56 * 1024 * 1024


def _params(sem):
    return pltpu.CompilerParams(dimension_semantics=sem, vmem_limit_bytes=VMEM_LIMIT)


def _rms_unit(x):
    return x * lax.rsqrt(jnp.mean(x * x, axis=-1, keepdims=True) + EPS)


def _sigmoid(x):
    return 0.5 * jnp.tanh(0.5 * x) + 0.5


def _silu(x):
    return x * _sigmoid(x)


def _resident(shape):
    nd = len(shape)
    return pl.BlockSpec(shape, lambda *_: (0,) * nd, pipeline_mode=pl.Buffered(1))


def _inproj_kernel(x_ref, g_ref, w_ref, u_ref, sg_ref, *rest, nu, tm, tn, tiles_per_seq,
                   group_width, fuse_pool):
    if fuse_pool:
        st_ref, h_ref, carry_ref = rest
    else:
        (h_ref,) = rest
    i = pl.program_id(0)
    j = pl.program_id(1)

    @pl.when(j == 0)
    def _():
        h_ref[...] = (_rms_unit(x_ref[...]) * g_ref[...]).astype(BF16)

    acc = jnp.dot(h_ref[...], w_ref[...], preferred_element_type=F32)

    @pl.when(j < nu)
    def _():
        if not fuse_pool:
            u_ref[...] = acc
            return
        seq_tile = i % tiles_per_seq
        tail = acc[tm - POOL_CARRY:, :]
        prev = jnp.where(seq_tile == 0, 0.0, carry_ref[j])
        ext = jnp.concatenate([prev, acc], axis=0)
        carry_ref[j] = tail
        st_ref[0] = tail
        pos = lax.broadcasted_iota(jnp.int32, (tm, 1), 0) + seq_tile * tm
        group = (j * tn) // group_width
        for gi, w in enumerate(POOL_WINDOWS):
            @pl.when(group == gi)
            def _(w=w):
                s, span = ext, 1
                while span < w:
                    s = s + pltpu.roll(s, span, axis=0)
                    span *= 2
                cnt = jnp.minimum(pos + 1, w).astype(F32)
                u_ref[...] = (s[POOL_CARRY:, :] / cnt - acc).astype(u_ref.dtype)

    @pl.when(j >= nu)
    def _():
        sg_ref[...] = _silu(acc).astype(sg_ref.dtype)


def _inproj(x, g, w, *, seq_len, fuse_pool):
    n, d = x.shape
    w2 = w.shape[1]
    wa = w2 // 2
    group_width = wa // len(POOL_WINDOWS)
    tn = min(1024, group_width)
    tm = min(1024, seq_len if fuse_pool else n)
    assert n % tm == 0 and wa % tn == 0 and group_width % tn == 0
    nu = wa // tn
    grid = (n // tm, 2 * nu)
    u_dtype = BF16 if fuse_pool else F32
    out_shape = [jax.ShapeDtypeStruct((n, wa), u_dtype), jax.ShapeDtypeStruct((n, wa), BF16)]
    out_specs = [pl.BlockSpec((tm, tn), lambda i, j: (i, jnp.minimum(j, nu - 1))),
                 pl.BlockSpec((tm, tn), lambda i, j: (i, jnp.maximum(j - nu, 0)))]
    scratch = [pltpu.VMEM((tm, d), BF16)]
    tiles_per_seq = 1
    if fuse_pool:
        assert seq_len % tm == 0 and tm >= POOL_CARRY
        tiles_per_seq = seq_len // tm
        out_shape.append(jax.ShapeDtypeStruct((n // tm, POOL_CARRY, wa), F32))
        out_specs.append(pl.BlockSpec((1, POOL_CARRY, tn),
                                      lambda i, j: (i, 0, jnp.minimum(j, nu - 1))))
        scratch += [pltpu.VMEM((nu, POOL_CARRY, tn), F32)]
    kern = functools.partial(_inproj_kernel, nu=nu, tm=tm, tn=tn, tiles_per_seq=tiles_per_seq,
                             group_width=group_width, fuse_pool=fuse_pool)
    return pl.pallas_call(
        kern, grid=grid,
        in_specs=[pl.BlockSpec((tm, d), lambda i, j: (i, 0)),
                  pl.BlockSpec((1, d), lambda i, j: (0, 0)),
                  pl.BlockSpec((d, tn), lambda i, j: (0, j))],
        out_specs=out_specs, out_shape=out_shape, scratch_shapes=scratch,
        compiler_params=_params(("arbitrary", "arbitrary")), name="inproj_pool",
    )(x, g, w)


def _pool_sample_kernel(ext_ref, o_ref, *, n_prev, t_new):
    group = pl.program_id(1)
    for gi, w in enumerate(POOL_WINDOWS):
        @pl.when(group == gi)
        def _(w=w):
            cur = ext_ref[:, n_prev:n_prev + t_new, :]
            s = cur
            for k in range(1, w):
                s = s + ext_ref[:, n_prev - k:n_prev - k + t_new, :]
            o_ref[...] = (s / float(w) - cur).astype(o_ref.dtype)


def _pool_sample(ext, t_new):
    s, rows, wa = ext.shape
    n_prev = rows - t_new
    assert n_prev >= max(POOL_WINDOWS) - 1
    gw = wa // len(POOL_WINDOWS)
    bs = min(16, s)
    assert s % bs == 0
    return pl.pallas_call(
        functools.partial(_pool_sample_kernel, n_prev=n_prev, t_new=t_new),
        grid=(s // bs, len(POOL_WINDOWS)),
        in_specs=[pl.BlockSpec((bs, rows, gw), lambda i, g: (i, 0, g))],
        out_specs=pl.BlockSpec((bs, t_new, gw), lambda i, g: (i, 0, g)),
        out_shape=jax.ShapeDtypeStruct((s, t_new, wa), BF16),
        compiler_params=_params(("arbitrary", "arbitrary")), name="pool_sample",
    )(ext)


def _group_kernel(p_ref, w_ref, scale_ref, sg_ref, y_ref):
    mixed = jnp.dot(p_ref[...], w_ref[0], preferred_element_type=F32)
    y_ref[...] = (mixed * scale_ref[...] * sg_ref[...].astype(F32)).astype(y_ref.dtype)


def _group_mix(pooled, w_group, scale, sgate):
    n, wa = pooled.shape
    ng, gw, _ = w_group.shape
    tm = min(1024, n)
    assert n % tm == 0
    return pl.pallas_call(
        _group_kernel, grid=(n // tm, ng),
        in_specs=[pl.BlockSpec((tm, gw), lambda i, g: (i, g)),
                  pl.BlockSpec((1, gw, gw), lambda i, g: (g, 0, 0)),
                  pl.BlockSpec((1, gw), lambda i, g: (0, g)),
                  pl.BlockSpec((tm, gw), lambda i, g: (i, g))],
        out_specs=pl.BlockSpec((tm, gw), lambda i, g: (i, g)),
        out_shape=jax.ShapeDtypeStruct((n, wa), BF16),
        compiler_params=_params(("arbitrary", "arbitrary")), name="group_mix",
    )(pooled, w_group, scale, sgate)


def _outproj_kernel(*refs, gated):
    if gated:
        y_ref, sg_ref, w_ref, x_ref, gpost_ref, gple_ref, x1_ref, hp_ref = refs
        y = (y_ref[...] * sg_ref[...].astype(F32)).astype(BF16)
    else:
        y_ref, w_ref, x_ref, gpost_ref, gple_ref, x1_ref, hp_ref = refs
        y = y_ref[...]
    out = jnp.dot(y, w_ref[...], preferred_element_type=F32)
    x1 = x_ref[...] + _rms_unit(out) * gpost_ref[...]
    x1_ref[...] = x1
    hp_ref[...] = (_rms_unit(x1) * gple_ref[...]).astype(BF16)


def _outproj(y, sgate, w_out, x, g_post, g_ple):
    n, k = y.shape
    d = x.shape[1]
    tm = min(512, n)
    assert n % tm == 0
    gated = sgate is not None
    row = lambda width: pl.BlockSpec((tm, width), lambda i: (i, 0))
    in_specs = [row(k)] + ([row(k)] if gated else []) + [
        _resident((k, d)), row(d), _resident((1, d)), _resident((1, d))]
    args = [y] + ([sgate] if gated else []) + [w_out, x, g_post, g_ple]
    return pl.pallas_call(
        functools.partial(_outproj_kernel, gated=gated), grid=(n // tm,),
        in_specs=in_specs, out_specs=[row(d), row(d)],
        out_shape=[jax.ShapeDtypeStruct((n, d), F32), jax.ShapeDtypeStruct((n, d), BF16)],
        compiler_params=_params(("arbitrary",)), name="outproj_post",
    )(*args)


def _ple_kernel(*refs, n_norm):
    hp_ref, wg_ref, p_ref, wp_ref, x1_ref = refs[:5]
    g_refs = refs[5:5 + n_norm]
    x2_ref = refs[5 + n_norm]
    h_refs = refs[6 + n_norm:]
    gate = _sigmoid(jnp.dot(hp_ref[...], wg_ref[...], preferred_element_type=F32))
    proj = jnp.dot(p_ref[...].astype(BF16), wp_ref[...], preferred_element_type=F32)
    x2 = x1_ref[...] + proj * gate
    x2_ref[...] = x2
    if n_norm:
        xn = _rms_unit(x2)
        for g_ref, h_ref in zip(g_refs, h_refs):
            h_ref[...] = (xn * g_ref[...]).astype(BF16)


def _ple(hp, w_gate, p, w_proj, x1, norm_gains):
    n, d = hp.shape
    pd = p.shape[1]
    tm = min(512, n)
    assert n % tm == 0
    n_norm = len(norm_gains)
    row = lambda width: pl.BlockSpec((tm, width), lambda i: (i, 0))
    in_specs = [row(d), _resident((d, d)), row(pd), _resident((pd, d)), row(d)] + \
               [_resident((1, d))] * n_norm
    out = pl.pallas_call(
        functools.partial(_ple_kernel, n_norm=n_norm), grid=(n // tm,),
        in_specs=in_specs, out_specs=[row(d)] * (1 + n_norm),
        out_shape=[jax.ShapeDtypeStruct((n, d), F32)] +
                  [jax.ShapeDtypeStruct((n, d), BF16)] * n_norm,
        compiler_params=_params(("arbitrary",)), name="ple",
    )(hp, w_gate, p, w_proj, x1, *norm_gains)
    return out[0], out[1:]


def _matmul_split_kernel(x_ref, w_ref, o0_ref, o1_ref, *, ns, f0, f1):
    j = pl.program_id(1)
    acc = jnp.dot(x_ref[...], w_ref[...], preferred_element_type=F32)

    @pl.when(j < ns)
    def _():
        o0_ref[...] = f0(acc).astype(o0_ref.dtype)

    @pl.when(j >= ns)
    def _():
        o1_ref[...] = f1(acc).astype(o1_ref.dtype)


def _matmul_split(x, w, dtypes, f0, f1, name):
    n, k = x.shape
    half = w.shape[1] // 2
    tm = min(1024, n)
    tn = min(1024, half)
    assert n % tm == 0 and half % tn == 0
    ns = half // tn
    return pl.pallas_call(
        functools.partial(_matmul_split_kernel, ns=ns, f0=f0, f1=f1), grid=(n // tm, 2 * ns),
        in_specs=[pl.BlockSpec((tm, k), lambda i, j: (i, 0)),
                  pl.BlockSpec((k, tn), lambda i, j: (0, j))],
        out_specs=[pl.BlockSpec((tm, tn), lambda i, j: (i, jnp.minimum(j, ns - 1))),
                   pl.BlockSpec((tm, tn), lambda i, j: (i, jnp.maximum(j - ns, 0)))],
        out_shape=[jax.ShapeDtypeStruct((n, half), dtypes[0]),
                   jax.ShapeDtypeStruct((n, half), dtypes[1])],
        compiler_params=_params(("arbitrary", "arbitrary")), name=name,
    )(x, w)


def _suffix_matrix(kb):
    assert kb & (kb - 1) == 0
    r = lax.broadcasted_iota(jnp.int32, (2 * kb, kb), 0) & (kb - 1)
    c = lax.broadcasted_iota(jnp.int32, (2 * kb, kb), 1)
    return jnp.where(r > c, 1.0, 0.0).astype(BF16)


def _add_rows(x, r0, delta):
    return x + delta if r0 == 0 else jnp.concatenate([x[:r0], x[r0:] + delta], axis=0)


def _sb_weights(z2s, masks, run, suffix, r0s):
    kb = z2s[0].shape[1]
    log_betas, neg_keeps, splits = [], [], []
    for z2, mask in zip(z2s, masks):
        neg_abs = lax.bitcast_convert_type(
            lax.bitcast_convert_type(z2, jnp.int32) | jnp.int32(-2 ** 31), F32)
        sp = jnp.log2(1.0 + jnp.exp2(neg_abs))
        log_beta = jnp.minimum(z2, 0.0) - sp
        neg_keep = z2 - log_beta
        if mask is not None:
            neg_keep = jnp.where(mask, neg_keep, 0.0)
        hi = neg_keep.astype(BF16)
        lo = (neg_keep - hi.astype(F32)).astype(BF16)
        log_betas.append(log_beta)
        neg_keeps.append(neg_keep)
        splits.append(jnp.concatenate([hi, lo], axis=1))
    sums = [jnp.dot(sp2, suffix, preferred_element_type=F32) for sp2 in splits]
    weights = []
    for log_beta, neg_keep, cs, mask, r0 in zip(log_betas, neg_keeps, sums, masks, r0s):
        later = jnp.tile(run[r0:], (1, kb // LANES)) + cs
        a = jnp.exp2(log_beta - later)
        if mask is not None:
            a = jnp.where(mask, a, 0.0)
        weights.append(a)
        run = _add_rows(run, r0, cs[:, 0:1] + neg_keep[:, 0:1])
    return weights, run


_NT = (((1,), (1,)), ((), ()))


def _sb_prompt_kernel(bias_ref, q_ref, k_ref, v_ref, sg_ref, o_ref,
                      kb_ref, vb_ref, sfx_ref, run_ref, acc_ref, *, seq_len):
    h = pl.program_id(1)
    kb_ref[...] = k_ref[0].astype(BF16)
    vb_ref[...] = v_ref[0].astype(BF16)
    sfx_ref[...] = _suffix_matrix(PROMPT_KB)
    z_scale = HEAD_DIM ** -0.5 * LOG2E
    z_bias = bias_ref[h] * LOG2E

    def blocks(row0, key0s, r0s, masks, kb, suffix):
        q = q_ref[0, pl.ds(pl.multiple_of(row0, PROMPT_TQ), PROMPT_TQ), :]
        keys = [pl.ds(pl.multiple_of(key0, kb), kb) for key0 in key0s]
        z2s = [lax.dot_general(q[r0:], kb_ref[k, :], _NT, preferred_element_type=F32) * z_scale
               + z_bias for k, r0 in zip(keys, r0s)]
        weights, run = _sb_weights(z2s, masks, run_ref[...], suffix, r0s)
        acc = acc_ref[...]
        for a, k, r0 in zip(weights, keys, r0s):
            acc = _add_rows(acc, r0, jnp.dot(a.astype(BF16), vb_ref[k, :],
                                             preferred_element_type=F32))
        run_ref[...] = run
        acc_ref[...] = acc

    def q_body(qi, _):
        row0 = qi * PROMPT_TQ
        run_ref[...] = jnp.zeros_like(run_ref)
        acc_ref[...] = jnp.zeros_like(acc_ref)
        dkb = PROMPT_DIAG_KB
        r0s = [c * dkb for c in reversed(range(PROMPT_TQ // dkb))]
        masks = [lax.broadcasted_iota(jnp.int32, (PROMPT_TQ - r0, dkb), 1) <
                 lax.broadcasted_iota(jnp.int32, (PROMPT_TQ - r0, dkb), 0) for r0 in r0s]
        diag_suffix = jnp.concatenate([sfx_ref[0:dkb, 0:dkb],
                                       sfx_ref[PROMPT_KB:PROMPT_KB + dkb, 0:dkb]], axis=0)
        blocks(row0, [row0 + r0 for r0 in r0s], r0s, masks, dkb, diag_suffix)
        n_full = qi * (PROMPT_TQ // PROMPT_KB)

        def k_body(t, _):
            first = n_full - 1 - t * PROMPT_UNROLL
            blocks(row0, [(first - c) * PROMPT_KB for c in range(PROMPT_UNROLL)],
                   [0] * PROMPT_UNROLL, [None] * PROMPT_UNROLL, PROMPT_KB, sfx_ref[...])
            return 0

        lax.fori_loop(0, n_full // PROMPT_UNROLL, k_body, 0)
        rows = pl.ds(pl.multiple_of(row0, PROMPT_TQ), PROMPT_TQ)
        o_ref[0, rows, :] = (acc_ref[...] * sg_ref[0, rows, :].astype(F32)).astype(o_ref.dtype)
        return 0

    lax.fori_loop(0, seq_len // PROMPT_TQ, q_body, 0)


def _sb_prompt(q, k, v, sgate, bias, n_heads):
    b, t, wb = q.shape
    assert t % PROMPT_TQ == 0 and wb == n_heads * HEAD_DIM
    assert (PROMPT_TQ // PROMPT_KB) % PROMPT_UNROLL == 0
    blk = pl.BlockSpec((1, t, HEAD_DIM), lambda bi, hi: (bi, 0, hi))
    return pl.pallas_call(
        functools.partial(_sb_prompt_kernel, seq_len=t), grid=(b, n_heads),
        in_specs=[pl.BlockSpec(memory_space=pltpu.SMEM), blk, blk, blk, blk],
        out_specs=blk,
        out_shape=jax.ShapeDtypeStruct((b, t, wb), BF16),
        scratch_shapes=[pltpu.VMEM((t, HEAD_DIM), BF16), pltpu.VMEM((t, HEAD_DIM), BF16),
                        pltpu.VMEM((2 * PROMPT_KB, PROMPT_KB), BF16),
                        pltpu.VMEM((PROMPT_TQ, LANES), F32),
                        pltpu.VMEM((PROMPT_TQ, HEAD_DIM), F32)],
        compiler_params=_params(("arbitrary", "arbitrary")), name="sb_prompt",
    )(bias, q, k, v, sgate)


def _sb_sample_kernel(pt_ref, q_ref, bias_ref, kn_ref, vn_ref, *rest, n_heads, t_new, pps):
    n_oct = n_heads // HEAD_OCTET
    n_page_refs = pps * n_oct
    k_refs, v_refs = rest[:n_page_refs], rest[n_page_refs:2 * n_page_refs]
    o_ref, knew_ref, vnew_ref, sfx_ref, run_ref, acc_ref = rest[2 * n_page_refs:]
    s = pl.program_id(0)
    p = pl.program_id(1)
    z_scale = HEAD_DIM ** -0.5 * LOG2E
    rows = n_heads * Q_PAD

    def block(k_heads, v_heads, mask, suffix):
        zs = [lax.dot_general(q_ref[0, h * Q_PAD:(h + 1) * Q_PAD, :], k_heads[h], _NT,
                              preferred_element_type=F32) for h in range(n_heads)]
        z_bias = jnp.tile(bias_ref[...] * LOG2E, (1, zs[0].shape[1] // LANES))
        z2 = jnp.concatenate(zs, axis=0) * z_scale + z_bias
        (a,), run = _sb_weights([z2], [mask], run_ref[...], suffix, [0])
        run_ref[...] = run
        a = a.astype(BF16)
        for h in range(n_heads):
            r = slice(h * Q_PAD, (h + 1) * Q_PAD)
            acc_ref[r, :] += jnp.dot(a[r, :], v_heads[h], preferred_element_type=F32)

    def page_heads(refs):
        out = []
        for h in range(n_heads):
            parts = []
            for c in range(pps):
                octet = refs[c * n_oct + h // HEAD_OCTET].at[0]
                flat = octet.reshape(KEY_BLOCK * HEAD_OCTET, HEAD_DIM)
                parts.append(flat[pl.ds(h % HEAD_OCTET, KEY_BLOCK, stride=HEAD_OCTET), :])
            out.append(jnp.concatenate(parts, axis=0).astype(BF16))
        return out

    @pl.when((s == 0) & (p == 0))
    def _():
        knew_ref[...] = jnp.zeros_like(knew_ref)
        vnew_ref[...] = jnp.zeros_like(vnew_ref)
        sfx_ref[...] = _suffix_matrix(pps * KEY_BLOCK)

    @pl.when(p == 0)
    def _():
        knew_ref[0:t_new, :] = kn_ref[0]
        vnew_ref[0:t_new, :] = vn_ref[0]
        run_ref[...] = jnp.zeros_like(run_ref)
        acc_ref[...] = jnp.zeros_like(acc_ref)
        qpos = lax.broadcasted_iota(jnp.int32, (rows, KEY_BLOCK), 0) % Q_PAD
        kpos = lax.broadcasted_iota(jnp.int32, (rows, KEY_BLOCK), 1)
        lanes = lambda ref: [ref[:, h * HEAD_DIM:(h + 1) * HEAD_DIM].astype(BF16)
                             for h in range(n_heads)]
        block(lanes(knew_ref), lanes(vnew_ref), kpos < qpos, _suffix_matrix(KEY_BLOCK))

    block(page_heads(k_refs), page_heads(v_refs), None, sfx_ref[...])

    @pl.when(p == pl.num_programs(1) - 1)
    def _():
        o_ref[0] = acc_ref[...]


def _sb_sample(q, bias_rows, k_new, v_new, cache_k, cache_v, page_table, n_heads):
    s, t_new, wb = k_new.shape
    n_pages = page_table.shape[1]
    pps = SAMPLE_PAGES_PER_STEP
    assert cache_k.shape[1:] == (KEY_BLOCK, n_heads, HEAD_DIM) and t_new < Q_PAD
    assert n_pages % pps == 0 and n_heads % HEAD_OCTET == 0
    n_oct = n_heads // HEAD_OCTET
    rows = n_heads * Q_PAD
    kb = pps * KEY_BLOCK

    def page(c, o):
        return pl.BlockSpec((1, KEY_BLOCK, HEAD_OCTET, HEAD_DIM),
                            lambda si, pi, pt: (pt[si, n_pages - pps * (pi + 1) + c], 0, o, 0))

    pages = [page(c, o) for c in range(pps) for o in range(n_oct)]
    new = pl.BlockSpec((1, t_new, wb), lambda si, pi, pt: (si, 0, 0))
    qo = pl.BlockSpec((1, rows, HEAD_DIM), lambda si, pi, pt: (si, 0, 0))
    bias_spec = pl.BlockSpec((rows, LANES), lambda si, pi, pt: (0, 0))
    grid_spec = pltpu.PrefetchScalarGridSpec(
        num_scalar_prefetch=1, grid=(s, n_pages // pps),
        in_specs=[qo, bias_spec, new, new] + pages + pages,
        out_specs=qo,
        scratch_shapes=[pltpu.VMEM((KEY_BLOCK, wb), F32), pltpu.VMEM((KEY_BLOCK, wb), F32),
                        pltpu.VMEM((2 * kb, kb), BF16),
                        pltpu.VMEM((rows, LANES), F32), pltpu.VMEM((rows, HEAD_DIM), F32)])
    caches = [cache_k] * len(pages) + [cache_v] * len(pages)
    return pl.pallas_call(
        functools.partial(_sb_sample_kernel, n_heads=n_heads, t_new=t_new, pps=pps),
        grid_spec=grid_spec,
        out_shape=jax.ShapeDtypeStruct((s, rows, HEAD_DIM), F32),
        compiler_params=_params(("arbitrary", "arbitrary")), name="sb_sample",
    )(page_table, q, bias_rows, k_new, v_new, *caches)


def _trunk(x, p, weights, past):
    (pre_g, post_g, a_w_in, a_w_group, a_scale, a_w_out, kv_g, w_kv, b_w_in, b_bias, b_w_out,
     ple_g, ple_w_gate, ple_w_proj) = weights
    b, t, d = x.shape
    n = b * t
    n_heads = b_bias.shape[1]
    wa = a_w_out.shape[1]
    xf = x.reshape(n, d)
    pf = p.reshape(p.shape[0], n, p.shape[-1])
    gain = lambda g: g.reshape(1, -1)

    if past is None:
        pooled, sgate, tails = _inproj(xf, gain(pre_g[0]), a_w_in[0], seq_len=t, fuse_pool=True)
        tails = tails.reshape(b, -1, POOL_CARRY, wa)[:, -1]
        pool_state = tails[:, POOL_CARRY - max(POOL_WINDOWS) + 1:, :]
    else:
        u, sgate = _inproj(xf, gain(pre_g[0]), a_w_in[0], seq_len=t, fuse_pool=False)
        ext = jnp.concatenate([past["pool"], u.reshape(b, t, wa)], axis=1)
        pooled = _pool_sample(ext, t).reshape(n, wa)
        pool_state = ext[:, t:, :]
    y = _group_mix(pooled, a_w_group[0], gain(a_scale[0]), sgate)
    x1, hp = _outproj(y, None, a_w_out[0], xf, gain(post_g[0]), gain(ple_g[0]))
    x2, (hk, hb) = _ple(hp, ple_w_gate[0], pf[0], ple_w_proj[0], x1, [gain(kv_g), gain(pre_g[1])])

    ident = lambda a: a
    k_new, v_new = _matmul_split(hk, w_kv, (F32, F32), ident, ident, "kv_proj")
    q, sgate_b = _matmul_split(hb, b_w_in[0], (BF16, BF16), ident, _silu, "q_proj")

    if past is None:
        shp = (b, t, n_heads * HEAD_DIM)
        y2 = _sb_prompt(q.reshape(shp), k_new.reshape(shp), v_new.reshape(shp),
                        sgate_b.reshape(shp), b_bias[0], n_heads).reshape(n, -1)
        x3, hp = _outproj(y2, None, b_w_out[0], x2, gain(post_g[1]), gain(ple_g[1]))
    else:
        q4 = q.reshape(b, t, n_heads, HEAD_DIM).transpose(0, 2, 1, 3)
        q4 = jnp.pad(q4, ((0, 0), (0, 0), (0, Q_PAD - t), (0, 0)))
        bias_rows = jnp.broadcast_to(jnp.repeat(b_bias[0], Q_PAD)[:, None],
                                     (n_heads * Q_PAD, LANES))
        o = _sb_sample(q4.reshape(b, n_heads * Q_PAD, HEAD_DIM), bias_rows,
                       k_new.reshape(b, t, -1), v_new.reshape(b, t, -1),
                       past["cache_k"], past["cache_v"], past["page_table"], n_heads)
        o = o.reshape(b, n_heads, Q_PAD, HEAD_DIM)[:, :, :t].transpose(0, 2, 1, 3).reshape(n, -1)
        x3, hp = _outproj(o, sgate_b, b_w_out[0], x2, gain(post_g[1]), gain(ple_g[1]))
    x4, _ = _ple(hp, ple_w_gate[1], pf[1], ple_w_proj[1], x3, [])

    kv_shape = (b, t, n_heads, HEAD_DIM)
    return (x4.reshape(b, t, d), k_new.reshape(kv_shape), v_new.reshape(kv_shape),
            pool_state[None])


def kernel(x_prompt, x_sample, cache_k, cache_v, state_pool, page_table, p_prompt, p_sample,
           pre_norm_g, post_norm_g, a_w_in, a_w_group, a_scale, a_w_out, kv_norm_g, w_kv,
           b_w_in, b_logit_bias, b_w_out, ple_norm_g, ple_w_gate, ple_w_proj):
    assert pre_norm_g.shape[0] == 2 and a_w_in.shape[0] == 1 and b_w_in.shape[0] == 1
    bf = lambda a: a.astype(BF16)
    weights = (pre_norm_g, post_norm_g, bf(a_w_in), bf(a_w_group), a_scale, bf(a_w_out),
               kv_norm_g, bf(w_kv), bf(b_w_in), b_logit_bias, bf(b_w_out), ple_norm_g,
               bf(ple_w_gate), bf(ple_w_proj))
    past = {"pool": state_pool[0], "cache_k": cache_k, "cache_v": cache_v,
            "page_table": page_table}
    y_p, k_p, v_p, st_p = _trunk(x_prompt, p_prompt, weights, None)
    y_s, k_s, v_s, st_s = _trunk(x_sample, p_sample, weights, past)
    return (y_p, y_s, k_p, v_p, k_s, v_s, st_p, st_s)
```

```python
import functools

import jax
import jax.numpy as jnp
from jax import lax
from jax.experimental import pallas as pl
from jax.experimental.pallas import tpu as pltpu

F32 = jnp.float32
BF16 = jnp.bfloat16

EPS = 1e-6
POOL_WINDOWS = (2, 4, 8, 16)
POOL_CARRY = 16
LANES = 128
HEAD_DIM = 128
HEAD_OCTET = 8
KEY_BLOCK = 128
Q_PAD = 16
PROMPT_TQ = 1024
PROMPT_KB = 256
PROMPT_DIAG_KB = 128
PROMPT_UNROLL = 4
SAMPLE_PAGES_PER_STEP = 4
SAMPLE_KB = 256
LOG2E = 1.4426950408889634
VMEM_LIMIT = 56 * 1024 * 1024


def _params(sem):
    return pltpu.CompilerParams(dimension_semantics=sem, vmem_limit_bytes=VMEM_LIMIT)


def _rms_unit(x):
    return x * lax.rsqrt(jnp.mean(x * x, axis=-1, keepdims=True) + EPS)


def _sigmoid(x):
    return 0.5 * jnp.tanh(0.5 * x) + 0.5


def _silu(x):
    return x * _sigmoid(x)


def _resident(shape):
    nd = len(shape)
    return pl.BlockSpec(shape, lambda *_: (0,) * nd, pipeline_mode=pl.Buffered(1))


def _inproj_kernel(x_ref, g_ref, w_ref, u_ref, sg_ref, *rest, nu, tm, tn, tiles_per_seq,
                   group_width, fuse_pool):
    if fuse_pool:
        st_ref, h_ref, carry_ref = rest
    else:
        (h_ref,) = rest
    i = pl.program_id(0)
    j = pl.program_id(1)

    @pl.when(j == 0)
    def _():
        h_ref[...] = (_rms_unit(x_ref[...]) * g_ref[...]).astype(BF16)

    acc = jnp.dot(h_ref[...], w_ref[...], preferred_element_type=F32)

    @pl.when(j < nu)
    def _():
        if not fuse_pool:
            u_ref[...] = acc
            return
        seq_tile = i % tiles_per_seq
        tail = acc[tm - POOL_CARRY:, :]
        prev = jnp.where(seq_tile == 0, 0.0, carry_ref[j])
        ext = jnp.concatenate([prev, acc], axis=0)
        carry_ref[j] = tail
        st_ref[0] = tail
        pos = lax.broadcasted_iota(jnp.int32, (tm, 1), 0) + seq_tile * tm
        group = (j * tn) // group_width
        for gi, w in enumerate(POOL_WINDOWS):
            @pl.when(group == gi)
            def _(w=w):
                s, span = ext, 1
                while span < w:
                    s = s + pltpu.roll(s, span, axis=0)
                    span *= 2
                cnt = jnp.minimum(pos + 1, w).astype(F32)
                u_ref[...] = (s[POOL_CARRY:, :] / cnt - acc).astype(u_ref.dtype)

    @pl.when(j >= nu)
    def _():
        sg_ref[...] = _silu(acc).astype(sg_ref.dtype)


def _inproj(x, g, w, *, seq_len, fuse_pool):
    n, d = x.shape
    w2 = w.shape[1]
    wa = w2 // 2
    group_width = wa // len(POOL_WINDOWS)
    tn = min(1024, group_width)
    tm = min(1024, seq_len if fuse_pool else n)
    assert n % tm == 0 and wa % tn == 0 and group_width % tn == 0
    nu = wa // tn
    grid = (n // tm, 2 * nu)
    u_dtype = BF16 if fuse_pool else F32
    out_shape = [jax.ShapeDtypeStruct((n, wa), u_dtype), jax.ShapeDtypeStruct((n, wa), BF16)]
    out_specs = [pl.BlockSpec((tm, tn), lambda i, j: (i, jnp.minimum(j, nu - 1))),
                 pl.BlockSpec((tm, tn), lambda i, j: (i, jnp.maximum(j - nu, 0)))]
    scratch = [pltpu.VMEM((tm, d), BF16)]
    tiles_per_seq = 1
    if fuse_pool:
        assert seq_len % tm == 0 and tm >= POOL_CARRY
        tiles_per_seq = seq_len // tm
        out_shape.append(jax.ShapeDtypeStruct((n // tm, POOL_CARRY, wa), F32))
        out_specs.append(pl.BlockSpec((1, POOL_CARRY, tn),
                                      lambda i, j: (i, 0, jnp.minimum(j, nu - 1))))
        scratch += [pltpu.VMEM((nu, POOL_CARRY, tn), F32)]
    kern = functools.partial(_inproj_kernel, nu=nu, tm=tm, tn=tn, tiles_per_seq=tiles_per_seq,
                             group_width=group_width, fuse_pool=fuse_pool)
    return pl.pallas_call(
        kern, grid=grid,
        in_specs=[pl.BlockSpec((tm, d), lambda i, j: (i, 0)),
                  pl.BlockSpec((1, d), lambda i, j: (0, 0)),
                  pl.BlockSpec((d, tn), lambda i, j: (0, j))],
        out_specs=out_specs, out_shape=out_shape, scratch_shapes=scratch,
        compiler_params=_params(("arbitrary", "arbitrary")), name="inproj_pool",
    )(x, g, w)


def _pool_sample_kernel(ext_ref, o_ref, *, n_prev, t_new):
    group = pl.program_id(1)
    for gi, w in enumerate(POOL_WINDOWS):
        @pl.when(group == gi)
        def _(w=w):
            cur = ext_ref[:, n_prev:n_prev + t_new, :]
            s = cur
            for k in range(1, w):
                s = s + ext_ref[:, n_prev - k:n_prev - k + t_new, :]
            o_ref[...] = (s / float(w) - cur).astype(o_ref.dtype)


def _pool_sample(ext, t_new):
    s, rows, wa = ext.shape
    n_prev = rows - t_new
    assert n_prev >= max(POOL_WINDOWS) - 1
    gw = wa // len(POOL_WINDOWS)
    bs = min(16, s)
    assert s % bs == 0
    return pl.pallas_call(
        functools.partial(_pool_sample_kernel, n_prev=n_prev, t_new=t_new),
        grid=(s // bs, len(POOL_WINDOWS)),
        in_specs=[pl.BlockSpec((bs, rows, gw), lambda i, g: (i, 0, g))],
        out_specs=pl.BlockSpec((bs, t_new, gw), lambda i, g: (i, 0, g)),
        out_shape=jax.ShapeDtypeStruct((s, t_new, wa), BF16),
        compiler_params=_params(("arbitrary", "arbitrary")), name="pool_sample",
    )(ext)


def _group_kernel(p_ref, w_ref, scale_ref, sg_ref, y_ref):
    mixed = jnp.dot(p_ref[...], w_ref[0], preferred_element_type=F32)
    y_ref[...] = (mixed * scale_ref[...] * sg_ref[...].astype(F32)).astype(y_ref.dtype)


def _group_mix(pooled, w_group, scale, sgate):
    n, wa = pooled.shape
    ng, gw, _ = w_group.shape
    tm = min(1024, n)
    assert n % tm == 0
    return pl.pallas_call(
        _group_kernel, grid=(n // tm, ng),
        in_specs=[pl.BlockSpec((tm, gw), lambda i, g: (i, g)),
                  pl.BlockSpec((1, gw, gw), lambda i, g: (g, 0, 0)),
                  pl.BlockSpec((1, gw), lambda i, g: (0, g)),
                  pl.BlockSpec((tm, gw), lambda i, g: (i, g))],
        out_specs=pl.BlockSpec((tm, gw), lambda i, g: (i, g)),
        out_shape=jax.ShapeDtypeStruct((n, wa), BF16),
        compiler_params=_params(("arbitrary", "arbitrary")), name="group_mix",
    )(pooled, w_group, scale, sgate)


def _outproj_kernel(*refs, gated):
    if gated:
        y_ref, sg_ref, w_ref, x_ref, gpost_ref, gple_ref, x1_ref, hp_ref = refs
        y = (y_ref[...] * sg_ref[...].astype(F32)).astype(BF16)
    else:
        y_ref, w_ref, x_ref, gpost_ref, gple_ref, x1_ref, hp_ref = refs
        y = y_ref[...]
    out = jnp.dot(y, w_ref[...], preferred_element_type=F32)
    x1 = x_ref[...] + _rms_unit(out) * gpost_ref[...]
    x1_ref[...] = x1
    hp_ref[...] = (_rms_unit(x1) * gple_ref[...]).astype(BF16)


def _outproj(y, sgate, w_out, x, g_post, g_ple):
    n, k = y.shape
    d = x.shape[1]
    tm = min(512, n)
    assert n % tm == 0
    gated = sgate is not None
    row = lambda width: pl.BlockSpec((tm, width), lambda i: (i, 0))
    in_specs = [row(k)] + ([row(k)] if gated else []) + [
        _resident((k, d)), row(d), _resident((1, d)), _resident((1, d))]
    args = [y] + ([sgate] if gated else []) + [w_out, x, g_post, g_ple]
    return pl.pallas_call(
        functools.partial(_outproj_kernel, gated=gated), grid=(n // tm,),
        in_specs=in_specs, out_specs=[row(d), row(d)],
        out_shape=[jax.ShapeDtypeStruct((n, d), F32), jax.ShapeDtypeStruct((n, d), BF16)],
        compiler_params=_params(("arbitrary",)), name="outproj_post",
    )(*args)


def _ple_kernel(*refs, n_norm):
    hp_ref, wg_ref, p_ref, wp_ref, x1_ref = refs[:5]
    g_refs = refs[5:5 + n_norm]
    x2_ref = refs[5 + n_norm]
    h_refs = refs[6 + n_norm:]
    gate = _sigmoid(jnp.dot(hp_ref[...], wg_ref[...], preferred_element_type=F32))
    proj = jnp.dot(p_ref[...].astype(BF16), wp_ref[...], preferred_element_type=F32)
    x2 = x1_ref[...] + proj * gate
    x2_ref[...] = x2
    if n_norm:
        xn = _rms_unit(x2)
        for g_ref, h_ref in zip(g_refs, h_refs):
            h_ref[...] = (xn * g_ref[...]).astype(BF16)


def _ple(hp, w_gate, p, w_proj, x1, norm_gains):
    n, d = hp.shape
    pd = p.shape[1]
    tm = min(512, n)
    assert n % tm == 0
    n_norm = len(norm_gains)
    row = lambda width: pl.BlockSpec((tm, width), lambda i: (i, 0))
    in_specs = [row(d), _resident((d, d)), row(pd), _resident((pd, d)), row(d)] + \
               [_resident((1, d))] * n_norm
    out = pl.pallas_call(
        functools.partial(_ple_kernel, n_norm=n_norm), grid=(n // tm,),
        in_specs=in_specs, out_specs=[row(d)] * (1 + n_norm),
        out_shape=[jax.ShapeDtypeStruct((n, d), F32)] +
                  [jax.ShapeDtypeStruct((n, d), BF16)] * n_norm,
        compiler_params=_params(("arbitrary",)), name="ple",
    )(hp, w_gate, p, w_proj, x1, *norm_gains)
    return out[0], out[1:]


def _matmul_split_kernel(x_ref, w_ref, o0_ref, o1_ref, *, ns, f0, f1):
    j = pl.program_id(1)
    acc = jnp.dot(x_ref[...], w_ref[...], preferred_element_type=F32)

    @pl.when(j < ns)
    def _():
        o0_ref[...] = f0(acc).astype(o0_ref.dtype)

    @pl.when(j >= ns)
    def _():
        o1_ref[...] = f1(acc).astype(o1_ref.dtype)


def _matmul_split(x, w, dtypes, f0, f1, name):
    n, k = x.shape
    half = w.shape[1] // 2
    tm = min(1024, n)
    tn = min(1024, half)
    assert n % tm == 0 and half % tn == 0
    ns = half // tn
    return pl.pallas_call(
        functools.partial(_matmul_split_kernel, ns=ns, f0=f0, f1=f1), grid=(n // tm, 2 * ns),
        in_specs=[pl.BlockSpec((tm, k), lambda i, j: (i, 0)),
                  pl.BlockSpec((k, tn), lambda i, j: (0, j))],
        out_specs=[pl.BlockSpec((tm, tn), lambda i, j: (i, jnp.minimum(j, ns - 1))),
                   pl.BlockSpec((tm, tn), lambda i, j: (i, jnp.maximum(j - ns, 0)))],
        out_shape=[jax.ShapeDtypeStruct((n, half), dtypes[0]),
                   jax.ShapeDtypeStruct((n, half), dtypes[1])],
        compiler_params=_params(("arbitrary", "arbitrary")), name=name,
    )(x, w)


def _suffix_matrix(kb):
    assert kb & (kb - 1) == 0
    r = lax.broadcasted_iota(jnp.int32, (2 * kb, kb), 0) & (kb - 1)
    c = lax.broadcasted_iota(jnp.int32, (2 * kb, kb), 1)
    return jnp.where(r > c, 1.0, 0.0).astype(BF16)


def _add_rows(x, r0, delta):
    return x + delta if r0 == 0 else jnp.concatenate([x[:r0], x[r0:] + delta], axis=0)


def _sb_weights(z2s, masks, run, suffix, r0s):
    kb = z2s[0].shape[1]
    log_betas, neg_keeps, splits = [], [], []
    for z2, mask in zip(z2s, masks):
        sp = jnp.log2(1.0 + jnp.exp2(-jnp.abs(z2)))
        log_beta = jnp.minimum(z2, 0.0) - sp
        neg_keep = z2 - log_beta
        if mask is not None:
            neg_keep = jnp.where(mask, neg_keep, 0.0)
        hi = neg_keep.astype(BF16)
        lo = (neg_keep - hi.astype(F32)).astype(BF16)
        log_betas.append(log_beta)
        neg_keeps.append(neg_keep)
        splits.append(jnp.concatenate([hi, lo], axis=1))
    sums = [jnp.dot(sp2, suffix, preferred_element_type=F32) for sp2 in splits]
    weights = []
    for log_beta, neg_keep, cs, mask, r0 in zip(log_betas, neg_keeps, sums, masks, r0s):
        later = jnp.tile(run[r0:], (1, kb // LANES)) + cs
        a = jnp.exp2(log_beta - later)
        if mask is not None:
            a = jnp.where(mask, a, 0.0)
        weights.append(a)
        run = _add_rows(run, r0, cs[:, 0:1] + neg_keep[:, 0:1])
    return weights, run


_NT = (((1,), (1,)), ((), ()))


def _sb_prompt_kernel(bias_ref, q_ref, k_ref, v_ref, sg_ref, o_ref,
                      kb_ref, vb_ref, sfx_ref, run_ref, acc_ref, *, seq_len):
    h = pl.program_id(1)
    kb_ref[...] = k_ref[0].astype(BF16)
    vb_ref[...] = v_ref[0].astype(BF16)
    sfx_ref[...] = _suffix_matrix(PROMPT_KB)
    z_bias = bias_ref[h] * LOG2E

    def blocks(row0, key0s, r0s, masks, kb, suffix):
        q = q_ref[0, pl.ds(pl.multiple_of(row0, PROMPT_TQ), PROMPT_TQ), :]
        keys = [pl.ds(pl.multiple_of(key0, kb), kb) for key0 in key0s]
        z2s = [lax.dot_general(q[r0:], kb_ref[k, :], _NT, preferred_element_type=F32) + z_bias
               for k, r0 in zip(keys, r0s)]
        weights, run = _sb_weights(z2s, masks, run_ref[...], suffix, r0s)
        acc = acc_ref[...]
        for a, k, r0 in zip(weights, keys, r0s):
            acc = _add_rows(acc, r0, jnp.dot(a.astype(BF16), vb_ref[k, :],
                                             preferred_element_type=F32))
        run_ref[...] = run
        acc_ref[...] = acc

    def q_body(qi, _):
        row0 = qi * PROMPT_TQ
        run_ref[...] = jnp.zeros_like(run_ref)
        acc_ref[...] = jnp.zeros_like(acc_ref)
        dkb = PROMPT_DIAG_KB
        r0s = [c * dkb for c in reversed(range(PROMPT_TQ // dkb))]
        masks = [lax.broadcasted_iota(jnp.int32, (PROMPT_TQ - r0, dkb), 1) <
                 lax.broadcasted_iota(jnp.int32, (PROMPT_TQ - r0, dkb), 0) for r0 in r0s]
        diag_suffix = jnp.concatenate([sfx_ref[0:dkb, 0:dkb],
                                       sfx_ref[PROMPT_KB:PROMPT_KB + dkb, 0:dkb]], axis=0)
        blocks(row0, [row0 + r0 for r0 in r0s], r0s, masks, dkb, diag_suffix)
        n_full = qi * (PROMPT_TQ // PROMPT_KB)

        def k_body(t, _):
            first = n_full - 1 - t * PROMPT_UNROLL
            blocks(row0, [(first - c) * PROMPT_KB for c in range(PROMPT_UNROLL)],
                   [0] * PROMPT_UNROLL, [None] * PROMPT_UNROLL, PROMPT_KB, sfx_ref[...])
            return 0

        lax.fori_loop(0, n_full // PROMPT_UNROLL, k_body, 0)
        rows = pl.ds(pl.multiple_of(row0, PROMPT_TQ), PROMPT_TQ)
        o_ref[0, rows, :] = (acc_ref[...] * sg_ref[0, rows, :].astype(F32)).astype(o_ref.dtype)
        return 0

    lax.fori_loop(0, seq_len // PROMPT_TQ, q_body, 0)


def _sb_prompt(q, k, v, sgate, bias, n_heads):
    b, t, wb = q.shape
    assert t % PROMPT_TQ == 0 and wb == n_heads * HEAD_DIM
    assert (PROMPT_TQ // PROMPT_KB) % PROMPT_UNROLL == 0
    blk = pl.BlockSpec((1, t, HEAD_DIM), lambda bi, hi: (bi, 0, hi))
    return pl.pallas_call(
        functools.partial(_sb_prompt_kernel, seq_len=t), grid=(b, n_heads),
        in_specs=[pl.BlockSpec(memory_space=pltpu.SMEM), blk, blk, blk, blk],
        out_specs=blk,
        out_shape=jax.ShapeDtypeStruct((b, t, wb), BF16),
        scratch_shapes=[pltpu.VMEM((t, HEAD_DIM), BF16), pltpu.VMEM((t, HEAD_DIM), BF16),
                        pltpu.VMEM((2 * PROMPT_KB, PROMPT_KB), BF16),
                        pltpu.VMEM((PROMPT_TQ, LANES), F32),
                        pltpu.VMEM((PROMPT_TQ, HEAD_DIM), F32)],
        compiler_params=_params(("arbitrary", "arbitrary")), name="sb_prompt",
    )(bias, q, k, v, sgate)


def _sb_sample_kernel(pt_ref, q_ref, bias_ref, kn_ref, vn_ref, *rest, n_heads, t_new, pps):
    n_oct = n_heads // HEAD_OCTET
    n_page_refs = pps * n_oct
    k_refs, v_refs = rest[:n_page_refs], rest[n_page_refs:2 * n_page_refs]
    o_ref, knew_ref, vnew_ref, sfx_ref, run_ref, acc_ref = rest[2 * n_page_refs:]
    s = pl.program_id(0)
    p = pl.program_id(1)
    rows = n_heads * Q_PAD
    pages_per_block = SAMPLE_KB // KEY_BLOCK

    def blocks(k_blocks, v_blocks, mask, suffix):
        kb = k_blocks[0][0].shape[0]
        z_bias = jnp.tile(bias_ref[...] * LOG2E, (1, kb // LANES))
        z2s = [jnp.concatenate(
            [lax.dot_general(q_ref[0, h * Q_PAD:(h + 1) * Q_PAD, :], k_heads[h], _NT,
                             preferred_element_type=F32) for h in range(n_heads)], axis=0) + z_bias
               for k_heads in k_blocks]
        nb = len(z2s)
        weights, run = _sb_weights(z2s, [mask] * nb, run_ref[...], suffix, [0] * nb)
        run_ref[...] = run
        a = jnp.concatenate([w.astype(BF16) for w in reversed(weights)], axis=1)
        for h in range(n_heads):
            r = slice(h * Q_PAD, (h + 1) * Q_PAD)
            v = jnp.concatenate([v_heads[h] for v_heads in reversed(v_blocks)], axis=0)
            acc_ref[r, :] += jnp.dot(a[r, :], v, preferred_element_type=F32)

    def page_heads(refs, block):
        out = []
        for h in range(n_heads):
            parts = []
            for c in range(block * pages_per_block, (block + 1) * pages_per_block):
                octet = refs[c * n_oct + h // HEAD_OCTET].at[0]
                flat = octet.reshape(KEY_BLOCK * HEAD_OCTET, HEAD_DIM)
                parts.append(flat[pl.ds(h % HEAD_OCTET, KEY_BLOCK, stride=HEAD_OCTET), :])
            out.append(jnp.concatenate(parts, axis=0).astype(BF16))
        return out

    @pl.when((s == 0) & (p == 0))
    def _():
        knew_ref[...] = jnp.zeros_like(knew_ref)
        vnew_ref[...] = jnp.zeros_like(vnew_ref)
        sfx_ref[...] = _suffix_matrix(SAMPLE_KB)

    @pl.when(p == 0)
    def _():
        knew_ref[0:t_new, :] = kn_ref[0]
        vnew_ref[0:t_new, :] = vn_ref[0]
        run_ref[...] = jnp.zeros_like(run_ref)
        acc_ref[...] = jnp.zeros_like(acc_ref)
        qpos = lax.broadcasted_iota(jnp.int32, (rows, KEY_BLOCK), 0) % Q_PAD
        kpos = lax.broadcasted_iota(jnp.int32, (rows, KEY_BLOCK), 1)
        lanes = lambda ref: [ref[:, h * HEAD_DIM:(h + 1) * HEAD_DIM].astype(BF16)
                             for h in range(n_heads)]
        blocks([lanes(knew_ref)], [lanes(vnew_ref)], kpos < qpos, _suffix_matrix(KEY_BLOCK))

    order = list(reversed(range(pps // pages_per_block)))
    blocks([page_heads(k_refs, b) for b in order], [page_heads(v_refs, b) for b in order],
           None, sfx_ref[...])

    @pl.when(p == pl.num_programs(1) - 1)
    def _():
        o_ref[0] = acc_ref[...]


def _sb_sample(q, bias_rows, k_new, v_new, cache_k, cache_v, page_table, n_heads):
    s, t_new, wb = k_new.shape
    n_pages = page_table.shape[1]
    pps = SAMPLE_PAGES_PER_STEP
    assert cache_k.shape[1:] == (KEY_BLOCK, n_heads, HEAD_DIM) and t_new < Q_PAD
    assert n_pages % pps == 0 and n_heads % HEAD_OCTET == 0
    n_oct = n_heads // HEAD_OCTET
    rows = n_heads * Q_PAD
    assert (pps * KEY_BLOCK) % SAMPLE_KB == 0 and SAMPLE_KB % KEY_BLOCK == 0

    def page(c, o):
        return pl.BlockSpec((1, KEY_BLOCK, HEAD_OCTET, HEAD_DIM),
                            lambda si, pi, pt: (pt[si, n_pages - pps * (pi + 1) + c], 0, o, 0))

    pages = [page(c, o) for c in range(pps) for o in range(n_oct)]
    new = pl.BlockSpec((1, t_new, wb), lambda si, pi, pt: (si, 0, 0))
    qo = pl.BlockSpec((1, rows, HEAD_DIM), lambda si, pi, pt: (si, 0, 0))
    bias_spec = pl.BlockSpec((rows, LANES), lambda si, pi, pt: (0, 0))
    grid_spec = pltpu.PrefetchScalarGridSpec(
        num_scalar_prefetch=1, grid=(s, n_pages // pps),
        in_specs=[qo, bias_spec, new, new] + pages + pages,
        out_specs=qo,
        scratch_shapes=[pltpu.VMEM((KEY_BLOCK, wb), F32), pltpu.VMEM((KEY_BLOCK, wb), F32),
                        pltpu.VMEM((2 * SAMPLE_KB, SAMPLE_KB), BF16),
                        pltpu.VMEM((rows, LANES), F32), pltpu.VMEM((rows, HEAD_DIM), F32)])
    caches = [cache_k] * len(pages) + [cache_v] * len(pages)
    return pl.pallas_call(
        functools.partial(_sb_sample_kernel, n_heads=n_heads, t_new=t_new, pps=pps),
        grid_spec=grid_spec,
        out_shape=jax.ShapeDtypeStruct((s, rows, HEAD_DIM), F32),
        compiler_params=_params(("arbitrary", "arbitrary")), name="sb_sample",
    )(page_table, q, bias_rows, k_new, v_new, *caches)


def _trunk(x, p, weights, past):
    (pre_g, post_g, a_w_in, a_w_group, a_scale, a_w_out, kv_g, w_kv, b_w_in, b_bias, b_w_out,
     ple_g, ple_w_gate, ple_w_proj) = weights
    b, t, d = x.shape
    n = b * t
    n_heads = b_bias.shape[1]
    wa = a_w_out.shape[1]
    xf = x.reshape(n, d)
    pf = p.reshape(p.shape[0], n, p.shape[-1])
    gain = lambda g: g.reshape(1, -1)

    if past is None:
        pooled, sgate, tails = _inproj(xf, gain(pre_g[0]), a_w_in[0], seq_len=t, fuse_pool=True)
        tails = tails.reshape(b, -1, POOL_CARRY, wa)[:, -1]
        pool_state = tails[:, POOL_CARRY - max(POOL_WINDOWS) + 1:, :]
    else:
        u, sgate = _inproj(xf, gain(pre_g[0]), a_w_in[0], seq_len=t, fuse_pool=False)
        ext = jnp.concatenate([past["pool"], u.reshape(b, t, wa)], axis=1)
        pooled = _pool_sample(ext, t).reshape(n, wa)
        pool_state = ext[:, t:, :]
    y = _group_mix(pooled, a_w_group[0], gain(a_scale[0]), sgate)
    x1, hp = _outproj(y, None, a_w_out[0], xf, gain(post_g[0]), gain(ple_g[0]))
    x2, (hk, hb) = _ple(hp, ple_w_gate[0], pf[0], ple_w_proj[0], x1, [gain(kv_g), gain(pre_g[1])])

    ident = lambda a: a
    k_new, v_new = _matmul_split(hk, w_kv, (F32, F32), ident, ident, "kv_proj")
    q_scale = HEAD_DIM ** -0.5 * LOG2E
    q, sgate_b = _matmul_split(hb, b_w_in[0], (BF16, BF16), lambda a: a * q_scale, _silu,
                               "q_proj")

    if past is None:
        shp = (b, t, n_heads * HEAD_DIM)
        y2 = _sb_prompt(q.reshape(shp), k_new.reshape(shp), v_new.reshape(shp),
                        sgate_b.reshape(shp), b_bias[0], n_heads).reshape(n, -1)
        x3, hp = _outproj(y2, None, b_w_out[0], x2, gain(post_g[1]), gain(ple_g[1]))
    else:
        q4 = q.reshape(b, t, n_heads, HEAD_DIM).transpose(0, 2, 1, 3)
        q4 = jnp.pad(q4, ((0, 0), (0, 0), (0, Q_PAD - t), (0, 0)))
        bias_rows = jnp.broadcast_to(jnp.repeat(b_bias[0], Q_PAD)[:, None],
                                     (n_heads * Q_PAD, LANES))
        o = _sb_sample(q4.reshape(b, n_heads * Q_PAD, HEAD_DIM), bias_rows,
                       k_new.reshape(b, t, -1), v_new.reshape(b, t, -1),
                       past["cache_k"], past["cache_v"], past["page_table"], n_heads)
        o = o.reshape(b, n_heads, Q_PAD, HEAD_DIM)[:, :, :t].transpose(0, 2, 1, 3).reshape(n, -1)
        x3, hp = _outproj(o, sgate_b, b_w_out[0], x2, gain(post_g[1]), gain(ple_g[1]))
    x4, _ = _ple(hp, ple_w_gate[1], pf[1], ple_w_proj[1], x3, [])

    kv_shape = (b, t, n_heads, HEAD_DIM)
    return (x4.reshape(b, t, d), k_new.reshape(kv_shape), v_new.reshape(kv_shape),
            pool_state[None])


def kernel(x_prompt, x_sample, cache_k, cache_v, state_pool, page_table, p_prompt, p_sample,
           pre_norm_g, post_norm_g, a_w_in, a_w_group, a_scale, a_w_out, kv_norm_g, w_kv,
           b_w_in, b_logit_bias, b_w_out, ple_norm_g, ple_w_gate, ple_w_proj):
    assert pre_norm_g.shape[0] == 2 and a_w_in.shape[0] == 1 and b_w_in.shape[0] == 1
    bf = lambda a: a.astype(BF16)
    weights = (pre_norm_g, post_norm_g, bf(a_w_in), bf(a_w_group), a_scale, bf(a_w_out),
               kv_norm_g, bf(w_kv), bf(b_w_in), b_logit_bias, bf(b_w_out), ple_norm_g,
               bf(ple_w_gate), bf(ple_w_proj))
    past = {"pool": state_pool[0], "cache_k": cache_k, "cache_v": cache_v,
            "page_table": page_table}
    y_p, k_p, v_p, st_p = _trunk(x_prompt, p_prompt, weights, None)
    y_s, k_s, v_s, st_s = _trunk(x_sample, p_sample, weights, past)
    return (y_p, y_s, k_p, v_p, k_s, v_s, st_p, st_s)
```

```python
import functools

import jax
import jax.numpy as jnp
from jax import lax
from jax.experimental import pallas as pl
from jax.experimental.pallas import tpu as pltpu

F32 = jnp.float32
BF16 = jnp.bfloat16

EPS = 1e-6
POOL_WINDOWS = (2, 4, 8, 16)
POOL_CARRY = 16
LANES = 128
HEAD_DIM = 128
HEAD_OCTET = 8
KEY_BLOCK = 128
Q_PAD = 16
PROMPT_TQ = 1024
PROMPT_KB = 256
PROMPT_DIAG_KB = 128
PROMPT_UNROLL = 4
SAMPLE_PAGES_PER_STEP = 4
SAMPLE_KB = 256
LOG2E = 1.4426950408889634
VMEM_LIMIT = 56 * 1024 * 1024


def _params(sem):
    return pltpu.CompilerParams(dimension_semantics=sem, vmem_limit_bytes=VMEM_LIMIT)


def _rms_unit(x):
    return x * lax.rsqrt(jnp.mean(x * x, axis=-1, keepdims=True) + EPS)


def _sigmoid(x):
    return 0.5 * jnp.tanh(0.5 * x) + 0.5


def _silu(x):
    return x * _sigmoid(x)


def _resident(shape):
    nd = len(shape)
    return pl.BlockSpec(shape, lambda *_: (0,) * nd, pipeline_mode=pl.Buffered(1))


def _inproj_kernel(x_ref, g_ref, w_ref, u_ref, sg_ref, *rest, nu, tm, tn, tiles_per_seq,
                   group_width, fuse_pool):
    if fuse_pool:
        st_ref, h_ref, carry_ref = rest
    else:
        (h_ref,) = rest
    i = pl.program_id(0)
    j = pl.program_id(1)

    @pl.when(j == 0)
    def _():
        h_ref[...] = (_rms_unit(x_ref[...]) * g_ref[...]).astype(BF16)

    n_split = 2 if fuse_pool else 1
    ts = tn // n_split

    def dot_part(c):
        return jnp.dot(h_ref[...], w_ref[:, c * ts:(c + 1) * ts], preferred_element_type=F32)

    if not fuse_pool:
        @pl.when(j < nu)
        def _():
            u_ref[...] = dot_part(0)
    else:
        seq_tile = i % tiles_per_seq
        pos = lax.broadcasted_iota(jnp.int32, (tm, 1), 0) + seq_tile * tm
        group = (j * tn) // group_width
        for gi, w in enumerate(POOL_WINDOWS):
            @pl.when((j < nu) & (group == gi))
            def _(w=w):
                for c in range(n_split):
                    cols = slice(c * ts, (c + 1) * ts)
                    acc = dot_part(c)
                    tail = acc[tm - POOL_CARRY:, :]
                    prev = jnp.where(seq_tile == 0, 0.0, carry_ref[j, :, cols])
                    carry_ref[j, :, cols] = tail
                    st_ref[0, :, cols] = tail
                    s, span = jnp.concatenate([prev, acc], axis=0), 1
                    while span < w:
                        s = s + pltpu.roll(s, span, axis=0)
                        span *= 2
                    cnt = jnp.minimum(pos + 1, w).astype(F32)
                    u_ref[:, cols] = (s[POOL_CARRY:, :] / cnt - acc).astype(u_ref.dtype)

    @pl.when(j >= nu)
    def _():
        for c in range(n_split):
            cols = slice(c * ts, (c + 1) * ts)
            sg_ref[:, cols] = _silu(dot_part(c)).astype(sg_ref.dtype)


def _inproj(x, g, w, *, seq_len, fuse_pool):
    n, d = x.shape
    w2 = w.shape[1]
    wa = w2 // 2
    group_width = wa // len(POOL_WINDOWS)
    tn = min(1024, group_width)
    tm = min(1024, seq_len if fuse_pool else n)
    assert n % tm == 0 and wa % tn == 0 and group_width % tn == 0
    nu = wa // tn
    grid = (n // tm, 2 * nu)
    u_dtype = BF16 if fuse_pool else F32
    out_shape = [jax.ShapeDtypeStruct((n, wa), u_dtype), jax.ShapeDtypeStruct((n, wa), BF16)]
    out_specs = [pl.BlockSpec((tm, tn), lambda i, j: (i, jnp.minimum(j, nu - 1))),
                 pl.BlockSpec((tm, tn), lambda i, j: (i, jnp.maximum(j - nu, 0)))]
    scratch = [pltpu.VMEM((tm, d), BF16)]
    tiles_per_seq = 1
    if fuse_pool:
        assert seq_len % tm == 0 and tm >= POOL_CARRY
        tiles_per_seq = seq_len // tm
        out_shape.append(jax.ShapeDtypeStruct((n // tm, POOL_CARRY, wa), F32))
        out_specs.append(pl.BlockSpec((1, POOL_CARRY, tn),
                                      lambda i, j: (i, 0, jnp.minimum(j, nu - 1))))
        scratch += [pltpu.VMEM((nu, POOL_CARRY, tn), F32)]
    kern = functools.partial(_inproj_kernel, nu=nu, tm=tm, tn=tn, tiles_per_seq=tiles_per_seq,
                             group_width=group_width, fuse_pool=fuse_pool)
    return pl.pallas_call(
        kern, grid=grid,
        in_specs=[pl.BlockSpec((tm, d), lambda i, j: (i, 0)),
                  pl.BlockSpec((1, d), lambda i, j: (0, 0)),
                  pl.BlockSpec((d, tn), lambda i, j: (0, j))],
        out_specs=out_specs, out_shape=out_shape, scratch_shapes=scratch,
        compiler_params=_params(("arbitrary", "arbitrary")), name="inproj_pool",
    )(x, g, w)


def _pool_sample_kernel(prev_ref, u_ref, o_ref, st_ref, ext_ref, *, n_prev, t_new):
    group = pl.program_id(1)
    ext_ref[:, 0:n_prev, :] = prev_ref[0]
    ext_ref[:, n_prev:n_prev + t_new, :] = u_ref[...]
    st_ref[0] = ext_ref[:, t_new:n_prev + t_new, :]
    for gi, w in enumerate(POOL_WINDOWS):
        @pl.when(group == gi)
        def _(w=w):
            cur = ext_ref[:, n_prev:n_prev + t_new, :]
            s = cur
            for k in range(1, w):
                s = s + ext_ref[:, n_prev - k:n_prev - k + t_new, :]
            o_ref[...] = (s / float(w) - cur).astype(o_ref.dtype)


def _pool_sample(prev, u):
    _, s, n_prev, wa = prev.shape
    t_new = u.shape[1]
    assert n_prev >= max(POOL_WINDOWS) - 1 and n_prev >= t_new
    gw = wa // len(POOL_WINDOWS)
    bs = min(16, s)
    assert s % bs == 0
    return pl.pallas_call(
        functools.partial(_pool_sample_kernel, n_prev=n_prev, t_new=t_new),
        grid=(s // bs, len(POOL_WINDOWS)),
        in_specs=[pl.BlockSpec((1, bs, n_prev, gw), lambda i, g: (0, i, 0, g)),
                  pl.BlockSpec((bs, t_new, gw), lambda i, g: (i, 0, g))],
        out_specs=[pl.BlockSpec((bs, t_new, gw), lambda i, g: (i, 0, g)),
                   pl.BlockSpec((1, bs, n_prev, gw), lambda i, g: (0, i, 0, g))],
        out_shape=[jax.ShapeDtypeStruct((s, t_new, wa), BF16),
                   jax.ShapeDtypeStruct(prev.shape, F32)],
        scratch_shapes=[pltpu.VMEM((bs, n_prev + t_new, gw), F32)],
        compiler_params=_params(("arbitrary", "arbitrary")), name="pool_sample",
    )(prev, u)


def _group_kernel(p_ref, w_ref, scale_ref, sg_ref, y_ref):
    mixed = jnp.dot(p_ref[...], w_ref[0], preferred_element_type=F32)
    y_ref[...] = (mixed * scale_ref[...] * sg_ref[...].astype(F32)).astype(y_ref.dtype)


def _group_mix(pooled, w_group, scale, sgate):
    n, wa = pooled.shape
    ng, gw, _ = w_group.shape
    tm = min(1024, n)
    assert n % tm == 0
    return pl.pallas_call(
        _group_kernel, grid=(n // tm, ng),
        in_specs=[pl.BlockSpec((tm, gw), lambda i, g: (i, g)),
                  pl.BlockSpec((1, gw, gw), lambda i, g: (g, 0, 0)),
                  pl.BlockSpec((1, gw), lambda i, g: (0, g)),
                  pl.BlockSpec((tm, gw), lambda i, g: (i, g))],
        out_specs=pl.BlockSpec((tm, gw), lambda i, g: (i, g)),
        out_shape=jax.ShapeDtypeStruct((n, wa), BF16),
        compiler_params=_params(("arbitrary", "arbitrary")), name="group_mix",
    )(pooled, w_group, scale, sgate)


def _outproj_kernel(*refs, gated):
    if gated:
        y_ref, sg_ref, w_ref, x_ref, gpost_ref, gple_ref, x1_ref, hp_ref = refs
        y = (y_ref[...] * sg_ref[...].astype(F32)).astype(BF16)
    else:
        y_ref, w_ref, x_ref, gpost_ref, gple_ref, x1_ref, hp_ref = refs
        y = y_ref[...]
    out = jnp.dot(y, w_ref[...], preferred_element_type=F32)
    x1 = x_ref[...] + _rms_unit(out) * gpost_ref[...]
    x1_ref[...] = x1
    hp_ref[...] = (_rms_unit(x1) * gple_ref[...]).astype(BF16)


def _outproj(y, sgate, w_out, x, g_post, g_ple):
    n, k = y.shape
    d = x.shape[1]
    tm = min(512, n)
    assert n % tm == 0
    gated = sgate is not None
    row = lambda width: pl.BlockSpec((tm, width), lambda i: (i, 0))
    in_specs = [row(k)] + ([row(k)] if gated else []) + [
        _resident((k, d)), row(d), _resident((1, d)), _resident((1, d))]
    args = [y] + ([sgate] if gated else []) + [w_out, x, g_post, g_ple]
    return pl.pallas_call(
        functools.partial(_outproj_kernel, gated=gated), grid=(n // tm,),
        in_specs=in_specs, out_specs=[row(d), row(d)],
        out_shape=[jax.ShapeDtypeStruct((n, d), F32), jax.ShapeDtypeStruct((n, d), BF16)],
        compiler_params=_params(("arbitrary",)), name="outproj_post",
    )(*args)


def _ple_kernel(*refs, n_norm):
    hp_ref, wg_ref, p_ref, wp_ref, x1_ref = refs[:5]
    g_refs = refs[5:5 + n_norm]
    x2_ref = refs[5 + n_norm]
    h_refs = refs[6 + n_norm:]
    gate = _sigmoid(jnp.dot(hp_ref[...], wg_ref[...], preferred_element_type=F32))
    proj = jnp.dot(p_ref[...].astype(BF16), wp_ref[...], preferred_element_type=F32)
    x2 = x1_ref[...] + proj * gate
    x2_ref[...] = x2
    if n_norm:
        xn = _rms_unit(x2)
        for g_ref, h_ref in zip(g_refs, h_refs):
            h_ref[...] = (xn * g_ref[...]).astype(BF16)


def _ple(hp, w_gate, p, w_proj, x1, norm_gains):
    n, d = hp.shape
    pd = p.shape[1]
    tm = min(512, n)
    assert n % tm == 0
    n_norm = len(norm_gains)
    row = lambda width: pl.BlockSpec((tm, width), lambda i: (i, 0))
    in_specs = [row(d), _resident((d, d)), row(pd), _resident((pd, d)), row(d)] + \
               [_resident((1, d))] * n_norm
    out = pl.pallas_call(
        functools.partial(_ple_kernel, n_norm=n_norm), grid=(n // tm,),
        in_specs=in_specs, out_specs=[row(d)] * (1 + n_norm),
        out_shape=[jax.ShapeDtypeStruct((n, d), F32)] +
                  [jax.ShapeDtypeStruct((n, d), BF16)] * n_norm,
        compiler_params=_params(("arbitrary",)), name="ple",
    )(hp, w_gate, p, w_proj, x1, *norm_gains)
    return out[0], out[1:]


def _matmul_split_kernel(x_ref, w_ref, o0_ref, o1_ref, *, ns, f0, f1):
    j = pl.program_id(1)
    acc = jnp.dot(x_ref[...], w_ref[...], preferred_element_type=F32)

    @pl.when(j < ns)
    def _():
        o0_ref[...] = f0(acc).astype(o0_ref.dtype)

    @pl.when(j >= ns)
    def _():
        o1_ref[...] = f1(acc).astype(o1_ref.dtype)


def _matmul_split(x, w, dtypes, f0, f1, name):
    n, k = x.shape
    half = w.shape[1] // 2
    tm = min(1024, n)
    tn = min(1024, half)
    assert n % tm == 0 and half % tn == 0
    ns = half // tn
    return pl.pallas_call(
        functools.partial(_matmul_split_kernel, ns=ns, f0=f0, f1=f1), grid=(n // tm, 2 * ns),
        in_specs=[pl.BlockSpec((tm, k), lambda i, j: (i, 0)),
                  pl.BlockSpec((k, tn), lambda i, j: (0, j))],
        out_specs=[pl.BlockSpec((tm, tn), lambda i, j: (i, jnp.minimum(j, ns - 1))),
                   pl.BlockSpec((tm, tn), lambda i, j: (i, jnp.maximum(j - ns, 0)))],
        out_shape=[jax.ShapeDtypeStruct((n, half), dtypes[0]),
                   jax.ShapeDtypeStruct((n, half), dtypes[1])],
        compiler_params=_params(("arbitrary", "arbitrary")), name=name,
    )(x, w)


def _suffix_matrix(kb):
    assert kb & (kb - 1) == 0
    r = lax.broadcasted_iota(jnp.int32, (2 * kb, kb), 0) & (kb - 1)
    c = lax.broadcasted_iota(jnp.int32, (2 * kb, kb), 1)
    return jnp.where(r > c, 1.0, 0.0).astype(BF16)


def _add_rows(x, r0, delta):
    return x + delta if r0 == 0 else jnp.concatenate([x[:r0], x[r0:] + delta], axis=0)


def _sb_weights(z2s, masks, run, suffix, r0s):
    kb = z2s[0].shape[1]
    log_betas, neg_keeps, splits = [], [], []
    for z2, mask in zip(z2s, masks):
        sp = jnp.log2(1.0 + jnp.exp2(-jnp.abs(z2)))
        log_beta = jnp.minimum(z2, 0.0) - sp
        neg_keep = z2 - log_beta
        if mask is not None:
            neg_keep = jnp.where(mask, neg_keep, 0.0)
        hi = neg_keep.astype(BF16)
        lo = (neg_keep - hi.astype(F32)).astype(BF16)
        log_betas.append(log_beta)
        neg_keeps.append(neg_keep)
        splits.append(jnp.concatenate([hi, lo], axis=1))
    sums = [jnp.dot(sp2, suffix, preferred_element_type=F32) for sp2 in splits]
    weights = []
    for log_beta, neg_keep, cs, mask, r0 in zip(log_betas, neg_keeps, sums, masks, r0s):
        later = jnp.tile(run[r0:], (1, kb // LANES)) + cs
        a = jnp.exp2(log_beta - later)
        if mask is not None:
            a = jnp.where(mask, a, 0.0)
        weights.append(a)
        run = _add_rows(run, r0, cs[:, 0:1] + neg_keep[:, 0:1])
    return weights, run


_NT = (((1,), (1,)), ((), ()))


def _sb_prompt_kernel(bias_ref, q_ref, k_ref, v_ref, sg_ref, o_ref,
                      kb_ref, vb_ref, sfx_ref, run_ref, acc_ref, *, seq_len):
    h = pl.program_id(1)
    kb_ref[...] = k_ref[0].astype(BF16)
    vb_ref[...] = v_ref[0].astype(BF16)
    sfx_ref[...] = _suffix_matrix(PROMPT_KB)
    z_bias = bias_ref[h] * LOG2E

    def blocks(row0, key0s, r0s, masks, kb, suffix):
        q = q_ref[0, pl.ds(pl.multiple_of(row0, PROMPT_TQ), PROMPT_TQ), :]
        keys = [pl.ds(pl.multiple_of(key0, kb), kb) for key0 in key0s]
        z2s = [lax.dot_general(q[r0:], kb_ref[k, :], _NT, preferred_element_type=F32) + z_bias
               for k, r0 in zip(keys, r0s)]
        weights, run = _sb_weights(z2s, masks, run_ref[...], suffix, r0s)
        acc = acc_ref[...]
        for a, k, r0 in zip(weights, keys, r0s):
            acc = _add_rows(acc, r0, jnp.dot(a.astype(BF16), vb_ref[k, :],
                                             preferred_element_type=F32))
        run_ref[...] = run
        acc_ref[...] = acc

    def q_body(qi, _):
        row0 = qi * PROMPT_TQ
        run_ref[...] = jnp.zeros_like(run_ref)
        acc_ref[...] = jnp.zeros_like(acc_ref)
        dkb = PROMPT_DIAG_KB
        r0s = [c * dkb for c in reversed(range(PROMPT_TQ // dkb))]
        masks = [lax.broadcasted_iota(jnp.int32, (PROMPT_TQ - r0, dkb), 1) <
                 lax.broadcasted_iota(jnp.int32, (PROMPT_TQ - r0, dkb), 0) for r0 in r0s]
        diag_suffix = jnp.concatenate([sfx_ref[0:dkb, 0:dkb],
                                       sfx_ref[PROMPT_KB:PROMPT_KB + dkb, 0:dkb]], axis=0)
        blocks(row0, [row0 + r0 for r0 in r0s], r0s, masks, dkb, diag_suffix)
        n_full = qi * (PROMPT_TQ // PROMPT_KB)

        def k_body(t, _):
            first = n_full - 1 - t * PROMPT_UNROLL
            blocks(row0, [(first - c) * PROMPT_KB for c in range(PROMPT_UNROLL)],
                   [0] * PROMPT_UNROLL, [None] * PROMPT_UNROLL, PROMPT_KB, sfx_ref[...])
            return 0

        lax.fori_loop(0, n_full // PROMPT_UNROLL, k_body, 0)
        rows = pl.ds(pl.multiple_of(row0, PROMPT_TQ), PROMPT_TQ)
        o_ref[0, rows, :] = (acc_ref[...] * sg_ref[0, rows, :].astype(F32)).astype(o_ref.dtype)
        return 0

    lax.fori_loop(0, seq_len // PROMPT_TQ, q_body, 0)


def _sb_prompt(q, k, v, sgate, bias, n_heads):
    b, t, wb = q.shape
    assert t % PROMPT_TQ == 0 and wb == n_heads * HEAD_DIM
    assert (PROMPT_TQ // PROMPT_KB) % PROMPT_UNROLL == 0
    blk = pl.BlockSpec((1, t, HEAD_DIM), lambda bi, hi: (bi, 0, hi))
    return pl.pallas_call(
        functools.partial(_sb_prompt_kernel, seq_len=t), grid=(b, n_heads),
        in_specs=[pl.BlockSpec(memory_space=pltpu.SMEM), blk, blk, blk, blk],
        out_specs=blk,
        out_shape=jax.ShapeDtypeStruct((b, t, wb), BF16),
        scratch_shapes=[pltpu.VMEM((t, HEAD_DIM), BF16), pltpu.VMEM((t, HEAD_DIM), BF16),
                        pltpu.VMEM((2 * PROMPT_KB, PROMPT_KB), BF16),
                        pltpu.VMEM((PROMPT_TQ, LANES), F32),
                        pltpu.VMEM((PROMPT_TQ, HEAD_DIM), F32)],
        compiler_params=_params(("arbitrary", "arbitrary")), name="sb_prompt",
    )(bias, q, k, v, sgate)


def _sb_sample_kernel(pt_ref, q_ref, bias_ref, kn_ref, vn_ref, *rest, n_heads, t_new, pps):
    n_oct = n_heads // HEAD_OCTET
    n_page_refs = pps * n_oct
    k_refs, v_refs = rest[:n_page_refs], rest[n_page_refs:2 * n_page_refs]
    o_ref, knew_ref, vnew_ref, sfx_ref, run_ref, acc_ref = rest[2 * n_page_refs:]
    s = pl.program_id(0)
    p = pl.program_id(1)
    rows = n_heads * Q_PAD
    pages_per_block = SAMPLE_KB // KEY_BLOCK

    def blocks(k_blocks, v_blocks, mask, suffix):
        kb = k_blocks[0][0].shape[0]
        z_bias = jnp.tile(bias_ref[...] * LOG2E, (1, kb // LANES))
        z2s = [jnp.concatenate(
            [lax.dot_general(q_ref[0, h * Q_PAD:(h + 1) * Q_PAD, :], k_heads[h], _NT,
                             preferred_element_type=F32) for h in range(n_heads)], axis=0) + z_bias
               for k_heads in k_blocks]
        nb = len(z2s)
        weights, run = _sb_weights(z2s, [mask] * nb, run_ref[...], suffix, [0] * nb)
        run_ref[...] = run
        a = jnp.concatenate([w.astype(BF16) for w in reversed(weights)], axis=1)
        for h in range(n_heads):
            r = slice(h * Q_PAD, (h + 1) * Q_PAD)
            v = jnp.concatenate([v_heads[h] for v_heads in reversed(v_blocks)], axis=0)
            acc_ref[r, :] += jnp.dot(a[r, :], v, preferred_element_type=F32)

    def page_heads(refs, block):
        out = []
        for h in range(n_heads):
            parts = []
            for c in range(block * pages_per_block, (block + 1) * pages_per_block):
                octet = refs[c * n_oct + h // HEAD_OCTET].at[0]
                flat = octet.reshape(KEY_BLOCK * HEAD_OCTET, HEAD_DIM)
                parts.append(flat[pl.ds(h % HEAD_OCTET, KEY_BLOCK, stride=HEAD_OCTET), :])
            out.append(jnp.concatenate(parts, axis=0).astype(BF16))
        return out

    @pl.when((s == 0) & (p == 0))
    def _():
        knew_ref[...] = jnp.zeros_like(knew_ref)
        vnew_ref[...] = jnp.zeros_like(vnew_ref)
        sfx_ref[...] = _suffix_matrix(SAMPLE_KB)

    @pl.when(p == 0)
    def _():
        knew_ref[0:t_new, :] = kn_ref[0]
        vnew_ref[0:t_new, :] = vn_ref[0]
        run_ref[...] = jnp.zeros_like(run_ref)
        acc_ref[...] = jnp.zeros_like(acc_ref)
        qpos = lax.broadcasted_iota(jnp.int32, (rows, KEY_BLOCK), 0) % Q_PAD
        kpos = lax.broadcasted_iota(jnp.int32, (rows, KEY_BLOCK), 1)
        lanes = lambda ref: [ref[:, h * HEAD_DIM:(h + 1) * HEAD_DIM].astype(BF16)
                             for h in range(n_heads)]
        blocks([lanes(knew_ref)], [lanes(vnew_ref)], kpos < qpos, _suffix_matrix(KEY_BLOCK))

    order = list(reversed(range(pps // pages_per_block)))
    blocks([page_heads(k_refs, b) for b in order], [page_heads(v_refs, b) for b in order],
           None, sfx_ref[...])

    @pl.when(p == pl.num_programs(1) - 1)
    def _():
        o_ref[0] = acc_ref[...]


def _sb_sample(q, bias_rows, k_new, v_new, cache_k, cache_v, page_table, n_heads):
    s, t_new, wb = k_new.shape
    n_pages = page_table.shape[1]
    pps = SAMPLE_PAGES_PER_STEP
    assert cache_k.shape[1:] == (KEY_BLOCK, n_heads, HEAD_DIM) and t_new < Q_PAD
    assert n_pages % pps == 0 and n_heads % HEAD_OCTET == 0
    n_oct = n_heads // HEAD_OCTET
    rows = n_heads * Q_PAD
    assert (pps * KEY_BLOCK) % SAMPLE_KB == 0 and SAMPLE_KB % KEY_BLOCK == 0

    def page(c, o):
        return pl.BlockSpec((1, KEY_BLOCK, HEAD_OCTET, HEAD_DIM),
                            lambda si, pi, pt: (pt[si, n_pages - pps * (pi + 1) + c], 0, o, 0))

    pages = [page(c, o) for c in range(pps) for o in range(n_oct)]
    new = pl.BlockSpec((1, t_new, wb), lambda si, pi, pt: (si, 0, 0))
    qo = pl.BlockSpec((1, rows, HEAD_DIM), lambda si, pi, pt: (si, 0, 0))
    bias_spec = pl.BlockSpec((rows, LANES), lambda si, pi, pt: (0, 0))
    grid_spec = pltpu.PrefetchScalarGridSpec(
        num_scalar_prefetch=1, grid=(s, n_pages // pps),
        in_specs=[qo, bias_spec, new, new] + pages + pages,
        out_specs=qo,
        scratch_shapes=[pltpu.VMEM((KEY_BLOCK, wb), F32), pltpu.VMEM((KEY_BLOCK, wb), F32),
                        pltpu.VMEM((2 * SAMPLE_KB, SAMPLE_KB), BF16),
                        pltpu.VMEM((rows, LANES), F32), pltpu.VMEM((rows, HEAD_DIM), F32)])
    caches = [cache_k] * len(pages) + [cache_v] * len(pages)
    return pl.pallas_call(
        functools.partial(_sb_sample_kernel, n_heads=n_heads, t_new=t_new, pps=pps),
        grid_spec=grid_spec,
        out_shape=jax.ShapeDtypeStruct((s, rows, HEAD_DIM), F32),
        compiler_params=_params(("arbitrary", "arbitrary")), name="sb_sample",
    )(page_table, q, bias_rows, k_new, v_new, *caches)


def _trunk(x, p, weights, past):
    (pre_g, post_g, a_w_in, a_w_group, a_scale, a_w_out, kv_g, w_kv, b_w_in, b_bias, b_w_out,
     ple_g, ple_w_gate, ple_w_proj) = weights
    b, t, d = x.shape
    n = b * t
    n_heads = b_bias.shape[1]
    wa = a_w_out.shape[1]
    xf = x.reshape(n, d)
    pf = p.reshape(p.shape[0], n, p.shape[-1])
    gain = lambda g: g.reshape(1, -1)

    if past is None:
        pooled, sgate, tails = _inproj(xf, gain(pre_g[0]), a_w_in[0], seq_len=t, fuse_pool=True)
        tails = tails.reshape(b, -1, POOL_CARRY, wa)[:, -1]
        pool_state = tails[None, :, POOL_CARRY - max(POOL_WINDOWS) + 1:, :]
    else:
        u, sgate = _inproj(xf, gain(pre_g[0]), a_w_in[0], seq_len=t, fuse_pool=False)
        pooled, pool_state = _pool_sample(past["pool"], u.reshape(b, t, wa))
        pooled = pooled.reshape(n, wa)
    y = _group_mix(pooled, a_w_group[0], gain(a_scale[0]), sgate)
    x1, hp = _outproj(y, None, a_w_out[0], xf, gain(post_g[0]), gain(ple_g[0]))
    x2, (hk, hb) = _ple(hp, ple_w_gate[0], pf[0], ple_w_proj[0], x1, [gain(kv_g), gain(pre_g[1])])

    ident = lambda a: a
    k_new, v_new = _matmul_split(hk, w_kv, (F32, F32), ident, ident, "kv_proj")
    q_scale = HEAD_DIM ** -0.5 * LOG2E
    q, sgate_b = _matmul_split(hb, b_w_in[0], (BF16, BF16), lambda a: a * q_scale, _silu,
                               "q_proj")

    if past is None:
        shp = (b, t, n_heads * HEAD_DIM)
        y2 = _sb_prompt(q.reshape(shp), k_new.reshape(shp), v_new.reshape(shp),
                        sgate_b.reshape(shp), b_bias[0], n_heads).reshape(n, -1)
        x3, hp = _outproj(y2, None, b_w_out[0], x2, gain(post_g[1]), gain(ple_g[1]))
    else:
        q4 = q.reshape(b, t, n_heads, HEAD_DIM).transpose(0, 2, 1, 3)
        q4 = jnp.pad(q4, ((0, 0), (0, 0), (0, Q_PAD - t), (0, 0)))
        bias_rows = jnp.broadcast_to(jnp.repeat(b_bias[0], Q_PAD)[:, None],
                                     (n_heads * Q_PAD, LANES))
        o = _sb_sample(q4.reshape(b, n_heads * Q_PAD, HEAD_DIM), bias_rows,
                       k_new.reshape(b, t, -1), v_new.reshape(b, t, -1),
                       past["cache_k"], past["cache_v"], past["page_table"], n_heads)
        o = o.reshape(b, n_heads, Q_PAD, HEAD_DIM)[:, :, :t].transpose(0, 2, 1, 3).reshape(n, -1)
        x3, hp = _outproj(o, sgate_b, b_w_out[0], x2, gain(post_g[1]), gain(ple_g[1]))
    x4, _ = _ple(hp, ple_w_gate[1], pf[1], ple_w_proj[1], x3, [])

    kv_shape = (b, t, n_heads, HEAD_DIM)
    return x4.reshape(b, t, d), k_new.reshape(kv_shape), v_new.reshape(kv_shape), pool_state


def kernel(x_prompt, x_sample, cache_k, cache_v, state_pool, page_table, p_prompt, p_sample,
           pre_norm_g, post_norm_g, a_w_in, a_w_group, a_scale, a_w_out, kv_norm_g, w_kv,
           b_w_in, b_logit_bias, b_w_out, ple_norm_g, ple_w_gate, ple_w_proj):
    assert pre_norm_g.shape[0] == 2 and a_w_in.shape[0] == 1 and b_w_in.shape[0] == 1
    bf = lambda a: a.astype(BF16)
    weights = (pre_norm_g, post_norm_g, bf(a_w_in), bf(a_w_group), a_scale, bf(a_w_out),
               kv_norm_g, bf(w_kv), bf(b_w_in), b_logit_bias, bf(b_w_out), ple_norm_g,
               bf(ple_w_gate), bf(ple_w_proj))
    past = {"pool": state_pool, "cache_k": cache_k, "cache_v": cache_v,
            "page_table": page_table}
    y_p, k_p, v_p, st_p = _trunk(x_prompt, p_prompt, weights, None)
    y_s, k_s, v_s, st_s = _trunk(x_sample, p_sample, weights, past)
    return (y_p, y_s, k_p, v_p, k_s, v_s, st_p, st_s)
```

```python
import functools

import jax
import jax.numpy as jnp
from jax import lax
from jax.experimental import pallas as pl
from jax.experimental.pallas import tpu as pltpu

F32 = jnp.float32
BF16 = jnp.bfloat16

EPS = 1e-6
POOL_WINDOWS = (2, 4, 8, 16)
POOL_CARRY = 16
LANES = 128
HEAD_DIM = 128
HEAD_OCTET = 8
KEY_BLOCK = 128
Q_PAD = 16
PROMPT_TQ = 1024
PROMPT_KB = 256
PROMPT_DIAG_KB = 128
PROMPT_UNROLL = 4
SAMPLE_PAGES_PER_STEP = 8
SAMPLE_KB = 256
LOG2E = 1.4426950408889634
VMEM_LIMIT = 56 * 1024 * 1024


def _params(sem):
    return pltpu.CompilerParams(dimension_semantics=sem, vmem_limit_bytes=VMEM_LIMIT)


def _rms_unit(x):
    return x * lax.rsqrt(jnp.mean(x * x, axis=-1, keepdims=True) + EPS)


def _sigmoid(x):
    return 0.5 * jnp.tanh(0.5 * x) + 0.5


def _silu(x):
    return x * _sigmoid(x)


def _resident(shape):
    nd = len(shape)
    return pl.BlockSpec(shape, lambda *_: (0,) * nd, pipeline_mode=pl.Buffered(1))


def _inproj_kernel(x_ref, g_ref, w_ref, u_ref, sg_ref, *rest, nu, tm, tn, tiles_per_seq,
                   group_width, fuse_pool):
    if fuse_pool:
        st_ref, h_ref, carry_ref = rest
    else:
        (h_ref,) = rest
    i = pl.program_id(0)
    j = pl.program_id(1)

    @pl.when(j == 0)
    def _():
        h_ref[...] = (_rms_unit(x_ref[...]) * g_ref[...]).astype(BF16)

    n_split = 2 if fuse_pool else 1
    ts = tn // n_split

    def dot_part(c):
        return jnp.dot(h_ref[...], w_ref[:, c * ts:(c + 1) * ts], preferred_element_type=F32)

    if not fuse_pool:
        @pl.when(j < nu)
        def _():
            u_ref[...] = dot_part(0)
    else:
        seq_tile = i % tiles_per_seq
        pos = lax.broadcasted_iota(jnp.int32, (tm, 1), 0) + seq_tile * tm
        group = (j * tn) // group_width
        for gi, w in enumerate(POOL_WINDOWS):
            @pl.when((j < nu) & (group == gi))
            def _(w=w):
                for c in range(n_split):
                    cols = slice(c * ts, (c + 1) * ts)
                    acc = dot_part(c)
                    tail = acc[tm - POOL_CARRY:, :]
                    prev = jnp.where(seq_tile == 0, 0.0, carry_ref[j, :, cols])
                    carry_ref[j, :, cols] = tail
                    st_ref[0, :, cols] = tail
                    s, span = jnp.concatenate([prev, acc], axis=0), 1
                    while span < w:
                        s = s + pltpu.roll(s, span, axis=0)
                        span *= 2
                    cnt = jnp.minimum(pos + 1, w).astype(F32)
                    u_ref[:, cols] = (s[POOL_CARRY:, :] / cnt - acc).astype(u_ref.dtype)

    @pl.when(j >= nu)
    def _():
        for c in range(n_split):
            cols = slice(c * ts, (c + 1) * ts)
            sg_ref[:, cols] = _silu(dot_part(c)).astype(sg_ref.dtype)


def _inproj(x, g, w, *, seq_len, fuse_pool):
    n, d = x.shape
    w2 = w.shape[1]
    wa = w2 // 2
    group_width = wa // len(POOL_WINDOWS)
    tn = min(1024, group_width)
    tm = min(1024, seq_len if fuse_pool else n)
    assert n % tm == 0 and wa % tn == 0 and group_width % tn == 0
    nu = wa // tn
    grid = (n // tm, 2 * nu)
    u_dtype = BF16 if fuse_pool else F32
    out_shape = [jax.ShapeDtypeStruct((n, wa), u_dtype), jax.ShapeDtypeStruct((n, wa), BF16)]
    out_specs = [pl.BlockSpec((tm, tn), lambda i, j: (i, jnp.minimum(j, nu - 1))),
                 pl.BlockSpec((tm, tn), lambda i, j: (i, jnp.maximum(j - nu, 0)))]
    scratch = [pltpu.VMEM((tm, d), BF16)]
    tiles_per_seq = 1
    if fuse_pool:
        assert seq_len % tm == 0 and tm >= POOL_CARRY
        tiles_per_seq = seq_len // tm
        out_shape.append(jax.ShapeDtypeStruct((n // tm, POOL_CARRY, wa), F32))
        out_specs.append(pl.BlockSpec((1, POOL_CARRY, tn),
                                      lambda i, j: (i, 0, jnp.minimum(j, nu - 1))))
        scratch += [pltpu.VMEM((nu, POOL_CARRY, tn), F32)]
    kern = functools.partial(_inproj_kernel, nu=nu, tm=tm, tn=tn, tiles_per_seq=tiles_per_seq,
                             group_width=group_width, fuse_pool=fuse_pool)
    return pl.pallas_call(
        kern, grid=grid,
        in_specs=[pl.BlockSpec((tm, d), lambda i, j: (i, 0)),
                  pl.BlockSpec((1, d), lambda i, j: (0, 0)),
                  pl.BlockSpec((d, tn), lambda i, j: (0, j))],
        out_specs=out_specs, out_shape=out_shape, scratch_shapes=scratch,
        compiler_params=_params(("arbitrary", "arbitrary")), name="inproj_pool",
    )(x, g, w)


def _pool_sample_kernel(prev_ref, u_ref, o_ref, st_ref, ext_ref, *, n_prev, t_new):
    group = pl.program_id(1)
    ext_ref[:, 0:n_prev, :] = prev_ref[0]
    ext_ref[:, n_prev:n_prev + t_new, :] = u_ref[...]
    st_ref[0] = ext_ref[:, t_new:n_prev + t_new, :]
    for gi, w in enumerate(POOL_WINDOWS):
        @pl.when(group == gi)
        def _(w=w):
            cur = ext_ref[:, n_prev:n_prev + t_new, :]
            s = cur
            for k in range(1, w):
                s = s + ext_ref[:, n_prev - k:n_prev - k + t_new, :]
            o_ref[...] = (s / float(w) - cur).astype(o_ref.dtype)


def _pool_sample(prev, u):
    _, s, n_prev, wa = prev.shape
    t_new = u.shape[1]
    assert n_prev >= max(POOL_WINDOWS) - 1 and n_prev >= t_new
    gw = wa // len(POOL_WINDOWS)
    bs = min(16, s)
    assert s % bs == 0
    return pl.pallas_call(
        functools.partial(_pool_sample_kernel, n_prev=n_prev, t_new=t_new),
        grid=(s // bs, len(POOL_WINDOWS)),
        in_specs=[pl.BlockSpec((1, bs, n_prev, gw), lambda i, g: (0, i, 0, g)),
                  pl.BlockSpec((bs, t_new, gw), lambda i, g: (i, 0, g))],
        out_specs=[pl.BlockSpec((bs, t_new, gw), lambda i, g: (i, 0, g)),
                   pl.BlockSpec((1, bs, n_prev, gw), lambda i, g: (0, i, 0, g))],
        out_shape=[jax.ShapeDtypeStruct((s, t_new, wa), BF16),
                   jax.ShapeDtypeStruct(prev.shape, F32)],
        scratch_shapes=[pltpu.VMEM((bs, n_prev + t_new, gw), F32)],
        compiler_params=_params(("arbitrary", "arbitrary")), name="pool_sample",
    )(prev, u)


def _group_kernel(p_ref, w_ref, scale_ref, sg_ref, y_ref):
    mixed = jnp.dot(p_ref[...], w_ref[0], preferred_element_type=F32)
    y_ref[...] = (mixed * scale_ref[...] * sg_ref[...].astype(F32)).astype(y_ref.dtype)


def _group_mix(pooled, w_group, scale, sgate):
    n, wa = pooled.shape
    ng, gw, _ = w_group.shape
    tm = min(2048, n)
    assert n % tm == 0
    return pl.pallas_call(
        _group_kernel, grid=(n // tm, ng),
        in_specs=[pl.BlockSpec((tm, gw), lambda i, g: (i, g)),
                  pl.BlockSpec((1, gw, gw), lambda i, g: (g, 0, 0)),
                  pl.BlockSpec((1, gw), lambda i, g: (0, g)),
                  pl.BlockSpec((tm, gw), lambda i, g: (i, g))],
        out_specs=pl.BlockSpec((tm, gw), lambda i, g: (i, g)),
        out_shape=jax.ShapeDtypeStruct((n, wa), BF16),
        compiler_params=_params(("arbitrary", "arbitrary")), name="group_mix",
    )(pooled, w_group, scale, sgate)


def _outproj_kernel(*refs, gated, n_parts):
    if gated:
        y_ref, sg_ref, w_ref, x_ref, gpost_ref, gple_ref, x1_ref, hp_ref = refs
    else:
        y_ref, w_ref, x_ref, gpost_ref, gple_ref, x1_ref, hp_ref = refs
    rows_per = y_ref.shape[0] // n_parts
    for c in range(n_parts):
        rows = slice(c * rows_per, (c + 1) * rows_per)
        y = y_ref[rows, :]
        if gated:
            y = (y * sg_ref[rows, :].astype(F32)).astype(BF16)
        out = jnp.dot(y, w_ref[...], preferred_element_type=F32)
        x1 = x_ref[rows, :] + _rms_unit(out) * gpost_ref[...]
        x1_ref[rows, :] = x1
        hp_ref[rows, :] = (_rms_unit(x1) * gple_ref[...]).astype(BF16)


def _outproj(y, sgate, w_out, x, g_post, g_ple):
    n, k = y.shape
    d = x.shape[1]
    tm = min(512, n)
    assert n % tm == 0
    gated = sgate is not None
    row = lambda width: pl.BlockSpec((tm, width), lambda i: (i, 0))
    in_specs = [row(k)] + ([row(k)] if gated else []) + [
        _resident((k, d)), row(d), _resident((1, d)), _resident((1, d))]
    args = [y] + ([sgate] if gated else []) + [w_out, x, g_post, g_ple]
    return pl.pallas_call(
        functools.partial(_outproj_kernel, gated=gated, n_parts=2 if tm % 32 == 0 else 1),
        grid=(n // tm,),
        in_specs=in_specs, out_specs=[row(d), row(d)],
        out_shape=[jax.ShapeDtypeStruct((n, d), F32), jax.ShapeDtypeStruct((n, d), BF16)],
        compiler_params=_params(("arbitrary",)), name="outproj_post",
    )(*args)


def _ple_kernel(*refs, n_norm, n_parts):
    hp_ref, wg_ref, p_ref, wp_ref, x1_ref = refs[:5]
    g_refs = refs[5:5 + n_norm]
    x2_ref = refs[5 + n_norm]
    h_refs = refs[6 + n_norm:]
    rows_per = hp_ref.shape[0] // n_parts
    for c in range(n_parts):
        rows = slice(c * rows_per, (c + 1) * rows_per)
        gate = _sigmoid(jnp.dot(hp_ref[rows, :], wg_ref[...], preferred_element_type=F32))
        proj = jnp.dot(p_ref[rows, :].astype(BF16), wp_ref[...], preferred_element_type=F32)
        x2 = x1_ref[rows, :] + proj * gate
        x2_ref[rows, :] = x2
        if n_norm:
            xn = _rms_unit(x2)
            for g_ref, h_ref in zip(g_refs, h_refs):
                h_ref[rows, :] = (xn * g_ref[...]).astype(BF16)


def _ple(hp, w_gate, p, w_proj, x1, norm_gains):
    n, d = hp.shape
    pd = p.shape[1]
    tm = min(512, n)
    assert n % tm == 0
    n_norm = len(norm_gains)
    row = lambda width: pl.BlockSpec((tm, width), lambda i: (i, 0))
    in_specs = [row(d), _resident((d, d)), row(pd), _resident((pd, d)), row(d)] + \
               [_resident((1, d))] * n_norm
    out = pl.pallas_call(
        functools.partial(_ple_kernel, n_norm=n_norm, n_parts=2 if tm % 32 == 0 else 1),
        grid=(n // tm,),
        in_specs=in_specs, out_specs=[row(d)] * (1 + n_norm),
        out_shape=[jax.ShapeDtypeStruct((n, d), F32)] +
                  [jax.ShapeDtypeStruct((n, d), BF16)] * n_norm,
        compiler_params=_params(("arbitrary",)), name="ple",
    )(hp, w_gate, p, w_proj, x1, *norm_gains)
    return out[0], out[1:]


def _matmul_split_kernel(x_ref, w_ref, o0_ref, o1_ref, *, ns, f0, f1, n_parts):
    j = pl.program_id(1)
    ts = w_ref.shape[1] // n_parts

    def emit(o_ref, f):
        for c in range(n_parts):
            cols = slice(c * ts, (c + 1) * ts)
            acc = jnp.dot(x_ref[...], w_ref[:, cols], preferred_element_type=F32)
            o_ref[:, cols] = f(acc).astype(o_ref.dtype)

    @pl.when(j < ns)
    def _():
        emit(o0_ref, f0)

    @pl.when(j >= ns)
    def _():
        emit(o1_ref, f1)


def _matmul_split(x, w, dtypes, f0, f1, name):
    n, k = x.shape
    half = w.shape[1] // 2
    tm = min(1024, n)
    tn = min(1024, half)
    assert n % tm == 0 and half % tn == 0
    ns = half // tn
    return pl.pallas_call(
        functools.partial(_matmul_split_kernel, ns=ns, f0=f0, f1=f1,
                          n_parts=2 if tn % (2 * LANES) == 0 else 1),
        grid=(n // tm, 2 * ns),
        in_specs=[pl.BlockSpec((tm, k), lambda i, j: (i, 0)),
                  pl.BlockSpec((k, tn), lambda i, j: (0, j))],
        out_specs=[pl.BlockSpec((tm, tn), lambda i, j: (i, jnp.minimum(j, ns - 1))),
                   pl.BlockSpec((tm, tn), lambda i, j: (i, jnp.maximum(j - ns, 0)))],
        out_shape=[jax.ShapeDtypeStruct((n, half), dtypes[0]),
                   jax.ShapeDtypeStruct((n, half), dtypes[1])],
        compiler_params=_params(("arbitrary", "arbitrary")), name=name,
    )(x, w)


def _suffix_matrix(kb):
    assert kb & (kb - 1) == 0
    r = lax.broadcasted_iota(jnp.int32, (2 * kb, kb), 0) & (kb - 1)
    c = lax.broadcasted_iota(jnp.int32, (2 * kb, kb), 1)
    return jnp.where(r > c, 1.0, 0.0).astype(BF16)


def _add_rows(x, r0, delta):
    return x + delta if r0 == 0 else jnp.concatenate([x[:r0], x[r0:] + delta], axis=0)


def _sb_weights(z2s, masks, run, suffix, r0s):
    kb = z2s[0].shape[1]
    log_betas, neg_keeps, splits = [], [], []
    for z2, mask in zip(z2s, masks):
        sp = jnp.log2(1.0 + jnp.exp2(-jnp.abs(z2)))
        log_beta = jnp.minimum(z2, 0.0) - sp
        neg_keep = z2 - log_beta
        if mask is not None:
            neg_keep = jnp.where(mask, neg_keep, 0.0)
        hi = neg_keep.astype(BF16)
        lo = (neg_keep - hi.astype(F32)).astype(BF16)
        log_betas.append(log_beta)
        neg_keeps.append(neg_keep)
        splits.append(jnp.concatenate([hi, lo], axis=1))
    sums = [jnp.dot(sp2, suffix, preferred_element_type=F32) for sp2 in splits]
    weights = []
    for log_beta, neg_keep, cs, mask, r0 in zip(log_betas, neg_keeps, sums, masks, r0s):
        later = jnp.tile(run[r0:], (1, kb // LANES)) + cs
        a = jnp.exp2(log_beta - later)
        if mask is not None:
            a = jnp.where(mask, a, 0.0)
        weights.append(a)
        run = _add_rows(run, r0, cs[:, 0:1] + neg_keep[:, 0:1])
    return weights, run


_NT = (((1,), (1,)), ((), ()))


def _sb_prompt_kernel(bias_ref, q_ref, k_ref, v_ref, sg_ref, o_ref,
                      kb_ref, vb_ref, sfx_ref, run_ref, acc_ref, *, seq_len):
    h = pl.program_id(1)
    kb_ref[...] = k_ref[0].astype(BF16)
    vb_ref[...] = v_ref[0].astype(BF16)
    sfx_ref[...] = _suffix_matrix(PROMPT_KB)
    z_bias = bias_ref[h] * LOG2E

    def blocks(row0, key0s, r0s, masks, kb, suffix):
        q = q_ref[0, pl.ds(pl.multiple_of(row0, PROMPT_TQ), PROMPT_TQ), :]
        keys = [pl.ds(pl.multiple_of(key0, kb), kb) for key0 in key0s]
        z2s = [lax.dot_general(q[r0:], kb_ref[k, :], _NT, preferred_element_type=F32) + z_bias
               for k, r0 in zip(keys, r0s)]
        weights, run = _sb_weights(z2s, masks, run_ref[...], suffix, r0s)
        acc = acc_ref[...]
        for a, k, r0 in zip(weights, keys, r0s):
            acc = _add_rows(acc, r0, jnp.dot(a.astype(BF16), vb_ref[k, :],
                                             preferred_element_type=F32))
        run_ref[...] = run
        acc_ref[...] = acc

    def q_body(qi, _):
        row0 = qi * PROMPT_TQ
        run_ref[...] = jnp.zeros_like(run_ref)
        acc_ref[...] = jnp.zeros_like(acc_ref)
        dkb = PROMPT_DIAG_KB
        r0s = [c * dkb for c in reversed(range(PROMPT_TQ // dkb))]
        masks = [lax.broadcasted_iota(jnp.int32, (PROMPT_TQ - r0, dkb), 1) <
                 lax.broadcasted_iota(jnp.int32, (PROMPT_TQ - r0, dkb), 0) for r0 in r0s]
        diag_suffix = jnp.concatenate([sfx_ref[0:dkb, 0:dkb],
                                       sfx_ref[PROMPT_KB:PROMPT_KB + dkb, 0:dkb]], axis=0)
        blocks(row0, [row0 + r0 for r0 in r0s], r0s, masks, dkb, diag_suffix)
        n_full = qi * (PROMPT_TQ // PROMPT_KB)

        def k_body(t, _):
            first = n_full - 1 - t * PROMPT_UNROLL
            blocks(row0, [(first - c) * PROMPT_KB for c in range(PROMPT_UNROLL)],
                   [0] * PROMPT_UNROLL, [None] * PROMPT_UNROLL, PROMPT_KB, sfx_ref[...])
            return 0

        lax.fori_loop(0, n_full // PROMPT_UNROLL, k_body, 0)
        rows = pl.ds(pl.multiple_of(row0, PROMPT_TQ), PROMPT_TQ)
        o_ref[0, rows, :] = (acc_ref[...] * sg_ref[0, rows, :].astype(F32)).astype(o_ref.dtype)
        return 0

    lax.fori_loop(0, seq_len // PROMPT_TQ, q_body, 0)


def _sb_prompt(q, k, v, sgate, bias, n_heads):
    b, t, wb = q.shape
    assert t % PROMPT_TQ == 0 and wb == n_heads * HEAD_DIM
    assert (PROMPT_TQ // PROMPT_KB) % PROMPT_UNROLL == 0
    blk = pl.BlockSpec((1, t, HEAD_DIM), lambda bi, hi: (bi, 0, hi))
    return pl.pallas_call(
        functools.partial(_sb_prompt_kernel, seq_len=t), grid=(b, n_heads),
        in_specs=[pl.BlockSpec(memory_space=pltpu.SMEM), blk, blk, blk, blk],
        out_specs=blk,
        out_shape=jax.ShapeDtypeStruct((b, t, wb), BF16),
        scratch_shapes=[pltpu.VMEM((t, HEAD_DIM), BF16), pltpu.VMEM((t, HEAD_DIM), BF16),
                        pltpu.VMEM((2 * PROMPT_KB, PROMPT_KB), BF16),
                        pltpu.VMEM((PROMPT_TQ, LANES), F32),
                        pltpu.VMEM((PROMPT_TQ, HEAD_DIM), F32)],
        compiler_params=_params(("arbitrary", "arbitrary")), name="sb_prompt",
    )(bias, q, k, v, sgate)


def _sb_sample_kernel(pt_ref, q_ref, bias_ref, kn_ref, vn_ref, *rest, n_heads, t_new, pps):
    n_oct = n_heads // HEAD_OCTET
    n_page_refs = pps * n_oct
    k_refs, v_refs = rest[:n_page_refs], rest[n_page_refs:2 * n_page_refs]
    o_ref, knew_ref, vnew_ref, sfx_ref, run_ref, acc_ref = rest[2 * n_page_refs:]
    s = pl.program_id(0)
    p = pl.program_id(1)
    rows = n_heads * Q_PAD
    pages_per_block = SAMPLE_KB // KEY_BLOCK

    def blocks(k_blocks, v_blocks, mask, suffix):
        kb = k_blocks[0][0].shape[0]
        z_bias = jnp.tile(bias_ref[...] * LOG2E, (1, kb // LANES))
        z2s = [jnp.concatenate(
            [lax.dot_general(q_ref[0, h * Q_PAD:(h + 1) * Q_PAD, :], k_heads[h], _NT,
                             preferred_element_type=F32) for h in range(n_heads)], axis=0) + z_bias
               for k_heads in k_blocks]
        nb = len(z2s)
        weights, run = _sb_weights(z2s, [mask] * nb, run_ref[...], suffix, [0] * nb)
        run_ref[...] = run
        a = jnp.concatenate([w.astype(BF16) for w in reversed(weights)], axis=1)
        for h in range(n_heads):
            r = slice(h * Q_PAD, (h + 1) * Q_PAD)
            v = jnp.concatenate([v_heads[h] for v_heads in reversed(v_blocks)], axis=0)
            acc_ref[r, :] += jnp.dot(a[r, :], v, preferred_element_type=F32)

    def page_heads(refs, block):
        out = []
        for h in range(n_heads):
            parts = []
            for c in range(block * pages_per_block, (block + 1) * pages_per_block):
                octet = refs[c * n_oct + h // HEAD_OCTET].at[0]
                flat = octet.reshape(KEY_BLOCK * HEAD_OCTET, HEAD_DIM)
                parts.append(flat[pl.ds(h % HEAD_OCTET, KEY_BLOCK, stride=HEAD_OCTET), :])
            out.append(jnp.concatenate(parts, axis=0).astype(BF16))
        return out

    @pl.when((s == 0) & (p == 0))
    def _():
        knew_ref[...] = jnp.zeros_like(knew_ref)
        vnew_ref[...] = jnp.zeros_like(vnew_ref)
        sfx_ref[...] = _suffix_matrix(SAMPLE_KB)

    @pl.when(p == 0)
    def _():
        knew_ref[0:t_new, :] = kn_ref[0]
        vnew_ref[0:t_new, :] = vn_ref[0]
        run_ref[...] = jnp.zeros_like(run_ref)
        acc_ref[...] = jnp.zeros_like(acc_ref)
        qpos = lax.broadcasted_iota(jnp.int32, (rows, KEY_BLOCK), 0) % Q_PAD
        kpos = lax.broadcasted_iota(jnp.int32, (rows, KEY_BLOCK), 1)
        lanes = lambda ref: [ref[:, h * HEAD_DIM:(h + 1) * HEAD_DIM].astype(BF16)
                             for h in range(n_heads)]
        blocks([lanes(knew_ref)], [lanes(vnew_ref)], kpos < qpos, _suffix_matrix(KEY_BLOCK))

    order = list(reversed(range(pps // pages_per_block)))
    blocks([page_heads(k_refs, b) for b in order], [page_heads(v_refs, b) for b in order],
           None, sfx_ref[...])

    @pl.when(p == pl.num_programs(1) - 1)
    def _():
        o_ref[0] = acc_ref[...]


def _sb_sample(q, bias_rows, k_new, v_new, cache_k, cache_v, page_table, n_heads):
    s, t_new, wb = k_new.shape
    n_pages = page_table.shape[1]
    pps = SAMPLE_PAGES_PER_STEP
    assert cache_k.shape[1:] == (KEY_BLOCK, n_heads, HEAD_DIM) and t_new < Q_PAD
    assert n_pages % pps == 0 and n_heads % HEAD_OCTET == 0
    n_oct = n_heads // HEAD_OCTET
    rows = n_heads * Q_PAD
    assert (pps * KEY_BLOCK) % SAMPLE_KB == 0 and SAMPLE_KB % KEY_BLOCK == 0

    def page(c, o):
        return pl.BlockSpec((1, KEY_BLOCK, HEAD_OCTET, HEAD_DIM),
                            lambda si, pi, pt: (pt[si, n_pages - pps * (pi + 1) + c], 0, o, 0))

    pages = [page(c, o) for c in range(pps) for o in range(n_oct)]
    new = pl.BlockSpec((1, t_new, wb), lambda si, pi, pt: (si, 0, 0))
    qo = pl.BlockSpec((1, rows, HEAD_DIM), lambda si, pi, pt: (si, 0, 0))
    bias_spec = pl.BlockSpec((rows, LANES), lambda si, pi, pt: (0, 0))
    grid_spec = pltpu.PrefetchScalarGridSpec(
        num_scalar_prefetch=1, grid=(s, n_pages // pps),
        in_specs=[qo, bias_spec, new, new] + pages + pages,
        out_specs=qo,
        scratch_shapes=[pltpu.VMEM((KEY_BLOCK, wb), F32), pltpu.VMEM((KEY_BLOCK, wb), F32),
                        pltpu.VMEM((2 * SAMPLE_KB, SAMPLE_KB), BF16),
                        pltpu.VMEM((rows, LANES), F32), pltpu.VMEM((rows, HEAD_DIM), F32)])
    caches = [cache_k] * len(pages) + [cache_v] * len(pages)
    return pl.pallas_call(
        functools.partial(_sb_sample_kernel, n_heads=n_heads, t_new=t_new, pps=pps),
        grid_spec=grid_spec,
        out_shape=jax.ShapeDtypeStruct((s, rows, HEAD_DIM), F32),
        compiler_params=_params(("arbitrary", "arbitrary")), name="sb_sample",
    )(page_table, q, bias_rows, k_new, v_new, *caches)


def _trunk(x, p, weights, past):
    (pre_g, post_g, a_w_in, a_w_group, a_scale, a_w_out, kv_g, w_kv, b_w_in, b_bias, b_w_out,
     ple_g, ple_w_gate, ple_w_proj) = weights
    b, t, d = x.shape
    n = b * t
    n_heads = b_bias.shape[1]
    wa = a_w_out.shape[1]
    xf = x.reshape(n, d)
    pf = p.reshape(p.shape[0], n, p.shape[-1])
    gain = lambda g: g.reshape(1, -1)

    if past is None:
        pooled, sgate, tails = _inproj(xf, gain(pre_g[0]), a_w_in[0], seq_len=t, fuse_pool=True)
        tails = tails.reshape(b, -1, POOL_CARRY, wa)[:, -1]
        pool_state = tails[None, :, POOL_CARRY - max(POOL_WINDOWS) + 1:, :]
    else:
        u, sgate = _inproj(xf, gain(pre_g[0]), a_w_in[0], seq_len=t, fuse_pool=False)
        pooled, pool_state = _pool_sample(past["pool"], u.reshape(b, t, wa))
        pooled = pooled.reshape(n, wa)
    y = _group_mix(pooled, a_w_group[0], gain(a_scale[0]), sgate)
    x1, hp = _outproj(y, None, a_w_out[0], xf, gain(post_g[0]), gain(ple_g[0]))
    x2, (hk, hb) = _ple(hp, ple_w_gate[0], pf[0], ple_w_proj[0], x1, [gain(kv_g), gain(pre_g[1])])

    ident = lambda a: a
    k_new, v_new = _matmul_split(hk, w_kv, (F32, F32), ident, ident, "kv_proj")
    q_scale = HEAD_DIM ** -0.5 * LOG2E
    q, sgate_b = _matmul_split(hb, b_w_in[0], (BF16, BF16), lambda a: a * q_scale, _silu,
                               "q_proj")

    if past is None:
        shp = (b, t, n_heads * HEAD_DIM)
        y2 = _sb_prompt(q.reshape(shp), k_new.reshape(shp), v_new.reshape(shp),
                        sgate_b.reshape(shp), b_bias[0], n_heads).reshape(n, -1)
        x3, hp = _outproj(y2, None, b_w_out[0], x2, gain(post_g[1]), gain(ple_g[1]))
    else:
        q4 = q.reshape(b, t, n_heads, HEAD_DIM).transpose(0, 2, 1, 3)
        q4 = jnp.pad(q4, ((0, 0), (0, 0), (0, Q_PAD - t), (0, 0)))
        bias_rows = jnp.broadcast_to(jnp.repeat(b_bias[0], Q_PAD)[:, None],
                                     (n_heads * Q_PAD, LANES))
        o = _sb_sample(q4.reshape(b, n_heads * Q_PAD, HEAD_DIM), bias_rows,
                       k_new.reshape(b, t, -1), v_new.reshape(b, t, -1),
                       past["cache_k"], past["cache_v"], past["page_table"], n_heads)
        o = o.reshape(b, n_heads, Q_PAD, HEAD_DIM)[:, :, :t].transpose(0, 2, 1, 3).reshape(n, -1)
        x3, hp = _outproj(o, sgate_b, b_w_out[0], x2, gain(post_g[1]), gain(ple_g[1]))
    x4, _ = _ple(hp, ple_w_gate[1], pf[1], ple_w_proj[1], x3, [])

    kv_shape = (b, t, n_heads, HEAD_DIM)
    return x4.reshape(b, t, d), k_new.reshape(kv_shape), v_new.reshape(kv_shape), pool_state


def kernel(x_prompt, x_sample, cache_k, cache_v, state_pool, page_table, p_prompt, p_sample,
           pre_norm_g, post_norm_g, a_w_in, a_w_group, a_scale, a_w_out, kv_norm_g, w_kv,
           b_w_in, b_logit_bias, b_w_out, ple_norm_g, ple_w_gate, ple_w_proj):
    assert pre_norm_g.shape[0] == 2 and a_w_in.shape[0] == 1 and b_w_in.shape[0] == 1
    bf = lambda a: a.astype(BF16)
    weights = (pre_norm_g, post_norm_g, bf(a_w_in), bf(a_w_group), a_scale, bf(a_w_out),
               kv_norm_g, bf(w_kv), bf(b_w_in), b_logit_bias, bf(b_w_out), ple_norm_g,
               bf(ple_w_gate), bf(ple_w_proj))
    past = {"pool": state_pool, "cache_k": cache_k, "cache_v": cache_v,
            "page_table": page_table}
    y_p, k_p, v_p, st_p = _trunk(x_prompt, p_prompt, weights, None)
    y_s, k_s, v_s, st_s = _trunk(x_sample, p_sample, weights, past)
    return (y_p, y_s, k_p, v_p, k_s, v_s, st_p, st_s)
```

```python
import functools

import jax
import jax.numpy as jnp
from jax import lax
from jax.experimental import pallas as pl
from jax.experimental.pallas import tpu as pltpu

F32 = jnp.float32
BF16 = jnp.bfloat16

EPS = 1e-6
POOL_WINDOWS = (2, 4, 8, 16)
POOL_CARRY = 16
LANES = 128
HEAD_DIM = 128
HEAD_OCTET = 8
KEY_BLOCK = 128
Q_PAD = 16
PROMPT_TQ = 1024
PROMPT_KB = 256
PROMPT_DIAG_KB = 128
PROMPT_UNROLL = 4
SAMPLE_PAGES_PER_STEP = 4
SAMPLE_RING = 3
SAMPLE_KB = 256
LOG2E = 1.4426950408889634
VMEM_LIMIT = 56 * 1024 * 1024


def _params(sem):
    return pltpu.CompilerParams(dimension_semantics=sem, vmem_limit_bytes=VMEM_LIMIT)


def _rms_unit(x):
    return x * lax.rsqrt(jnp.mean(x * x, axis=-1, keepdims=True) + EPS)


def _sigmoid(x):
    return 0.5 * jnp.tanh(0.5 * x) + 0.5


def _silu(x):
    return x * _sigmoid(x)


def _resident(shape):
    nd = len(shape)
    return pl.BlockSpec(shape, lambda *_: (0,) * nd, pipeline_mode=pl.Buffered(1))


def _inproj_kernel(x_ref, g_ref, w_ref, u_ref, sg_ref, *rest, nu, tm, tn, tiles_per_seq,
                   group_width, fuse_pool):
    if fuse_pool:
        st_ref, h_ref, carry_ref = rest
    else:
        (h_ref,) = rest
    i = pl.program_id(0)
    j = pl.program_id(1)

    @pl.when(j == 0)
    def _():
        h_ref[...] = (_rms_unit(x_ref[...]) * g_ref[...]).astype(BF16)

    n_split = 2 if fuse_pool else 1
    ts = tn // n_split

    def dot_part(c):
        return jnp.dot(h_ref[...], w_ref[:, c * ts:(c + 1) * ts], preferred_element_type=F32)

    if not fuse_pool:
        @pl.when(j < nu)
        def _():
            u_ref[...] = dot_part(0)
    else:
        seq_tile = i % tiles_per_seq
        pos = lax.broadcasted_iota(jnp.int32, (tm, 1), 0) + seq_tile * tm
        group = (j * tn) // group_width
        for gi, w in enumerate(POOL_WINDOWS):
            @pl.when((j < nu) & (group == gi))
            def _(w=w):
                for c in range(n_split):
                    cols = slice(c * ts, (c + 1) * ts)
                    acc = dot_part(c)
                    tail = acc[tm - POOL_CARRY:, :]
                    prev = jnp.where(seq_tile == 0, 0.0, carry_ref[j, :, cols])
                    carry_ref[j, :, cols] = tail
                    st_ref[0, :, cols] = tail
                    s, span = jnp.concatenate([prev, acc], axis=0), 1
                    while span < w:
                        s = s + pltpu.roll(s, span, axis=0)
                        span *= 2
                    cnt = jnp.minimum(pos + 1, w).astype(F32)
                    u_ref[:, cols] = (s[POOL_CARRY:, :] / cnt - acc).astype(u_ref.dtype)

    @pl.when(j >= nu)
    def _():
        for c in range(n_split):
            cols = slice(c * ts, (c + 1) * ts)
            sg_ref[:, cols] = _silu(dot_part(c)).astype(sg_ref.dtype)


def _inproj(x, g, w, *, seq_len, fuse_pool):
    n, d = x.shape
    w2 = w.shape[1]
    wa = w2 // 2
    group_width = wa // len(POOL_WINDOWS)
    tn = min(1024, group_width)
    tm = min(1024, seq_len if fuse_pool else n)
    assert n % tm == 0 and wa % tn == 0 and group_width % tn == 0
    nu = wa // tn
    grid = (n // tm, 2 * nu)
    u_dtype = BF16 if fuse_pool else F32
    out_shape = [jax.ShapeDtypeStruct((n, wa), u_dtype), jax.ShapeDtypeStruct((n, wa), BF16)]
    out_specs = [pl.BlockSpec((tm, tn), lambda i, j: (i, jnp.minimum(j, nu - 1))),
                 pl.BlockSpec((tm, tn), lambda i, j: (i, jnp.maximum(j - nu, 0)))]
    scratch = [pltpu.VMEM((tm, d), BF16)]
    tiles_per_seq = 1
    if fuse_pool:
        assert seq_len % tm == 0 and tm >= POOL_CARRY
        tiles_per_seq = seq_len // tm
        out_shape.append(jax.ShapeDtypeStruct((n // tm, POOL_CARRY, wa), F32))
        out_specs.append(pl.BlockSpec((1, POOL_CARRY, tn),
                                      lambda i, j: (i, 0, jnp.minimum(j, nu - 1))))
        scratch += [pltpu.VMEM((nu, POOL_CARRY, tn), F32)]
    kern = functools.partial(_inproj_kernel, nu=nu, tm=tm, tn=tn, tiles_per_seq=tiles_per_seq,
                             group_width=group_width, fuse_pool=fuse_pool)
    return pl.pallas_call(
        kern, grid=grid,
        in_specs=[pl.BlockSpec((tm, d), lambda i, j: (i, 0)),
                  pl.BlockSpec((1, d), lambda i, j: (0, 0)),
                  pl.BlockSpec((d, tn), lambda i, j: (0, j))],
        out_specs=out_specs, out_shape=out_shape, scratch_shapes=scratch,
        compiler_params=_params(("arbitrary", "arbitrary")), name="inproj_pool",
    )(x, g, w)


def _pool_sample_kernel(prev_ref, u_ref, o_ref, st_ref, ext_ref, *, n_prev, t_new):
    group = pl.program_id(1)
    ext_ref[:, 0:n_prev, :] = prev_ref[0]
    ext_ref[:, n_prev:n_prev + t_new, :] = u_ref[...]
    st_ref[0] = ext_ref[:, t_new:n_prev + t_new, :]
    for gi, w in enumerate(POOL_WINDOWS):
        @pl.when(group == gi)
        def _(w=w):
            cur = ext_ref[:, n_prev:n_prev + t_new, :]
            s = cur
            for k in range(1, w):
                s = s + ext_ref[:, n_prev - k:n_prev - k + t_new, :]
            o_ref[...] = (s / float(w) - cur).astype(o_ref.dtype)


def _pool_sample(prev, u):
    _, s, n_prev, wa = prev.shape
    t_new = u.shape[1]
    assert n_prev >= max(POOL_WINDOWS) - 1 and n_prev >= t_new
    gw = wa // len(POOL_WINDOWS)
    bs = min(16, s)
    assert s % bs == 0
    return pl.pallas_call(
        functools.partial(_pool_sample_kernel, n_prev=n_prev, t_new=t_new),
        grid=(s // bs, len(POOL_WINDOWS)),
        in_specs=[pl.BlockSpec((1, bs, n_prev, gw), lambda i, g: (0, i, 0, g)),
                  pl.BlockSpec((bs, t_new, gw), lambda i, g: (i, 0, g))],
        out_specs=[pl.BlockSpec((bs, t_new, gw), lambda i, g: (i, 0, g)),
                   pl.BlockSpec((1, bs, n_prev, gw), lambda i, g: (0, i, 0, g))],
        out_shape=[jax.ShapeDtypeStruct((s, t_new, wa), BF16),
                   jax.ShapeDtypeStruct(prev.shape, F32)],
        scratch_shapes=[pltpu.VMEM((bs, n_prev + t_new, gw), F32)],
        compiler_params=_params(("arbitrary", "arbitrary")), name="pool_sample",
    )(prev, u)


def _group_kernel(p_ref, w_ref, scale_ref, sg_ref, y_ref):
    mixed = jnp.dot(p_ref[...], w_ref[0], preferred_element_type=F32)
    y_ref[...] = (mixed * scale_ref[...] * sg_ref[...].astype(F32)).astype(y_ref.dtype)


def _group_mix(pooled, w_group, scale, sgate):
    n, wa = pooled.shape
    ng, gw, _ = w_group.shape
    tm = min(2048, n)
    assert n % tm == 0
    return pl.pallas_call(
        _group_kernel, grid=(n // tm, ng),
        in_specs=[pl.BlockSpec((tm, gw), lambda i, g: (i, g)),
                  pl.BlockSpec((1, gw, gw), lambda i, g: (g, 0, 0)),
                  pl.BlockSpec((1, gw), lambda i, g: (0, g)),
                  pl.BlockSpec((tm, gw), lambda i, g: (i, g))],
        out_specs=pl.BlockSpec((tm, gw), lambda i, g: (i, g)),
        out_shape=jax.ShapeDtypeStruct((n, wa), BF16),
        compiler_params=_params(("arbitrary", "arbitrary")), name="group_mix",
    )(pooled, w_group, scale, sgate)


def _outproj_kernel(*refs, gated, n_parts):
    if gated:
        y_ref, sg_ref, w_ref, x_ref, gpost_ref, gple_ref, x1_ref, hp_ref = refs
    else:
        y_ref, w_ref, x_ref, gpost_ref, gple_ref, x1_ref, hp_ref = refs
    rows_per = y_ref.shape[0] // n_parts
    for c in range(n_parts):
        rows = slice(c * rows_per, (c + 1) * rows_per)
        y = y_ref[rows, :]
        if gated:
            y = (y * sg_ref[rows, :].astype(F32)).astype(BF16)
        out = jnp.dot(y, w_ref[...], preferred_element_type=F32)
        x1 = x_ref[rows, :] + _rms_unit(out) * gpost_ref[...]
        x1_ref[rows, :] = x1
        hp_ref[rows, :] = (_rms_unit(x1) * gple_ref[...]).astype(BF16)


def _outproj(y, sgate, w_out, x, g_post, g_ple):
    n, k = y.shape
    d = x.shape[1]
    tm = min(512, n)
    assert n % tm == 0
    gated = sgate is not None
    row = lambda width: pl.BlockSpec((tm, width), lambda i: (i, 0))
    in_specs = [row(k)] + ([row(k)] if gated else []) + [
        _resident((k, d)), row(d), _resident((1, d)), _resident((1, d))]
    args = [y] + ([sgate] if gated else []) + [w_out, x, g_post, g_ple]
    return pl.pallas_call(
        functools.partial(_outproj_kernel, gated=gated, n_parts=2 if tm % 32 == 0 else 1),
        grid=(n // tm,),
        in_specs=in_specs, out_specs=[row(d), row(d)],
        out_shape=[jax.ShapeDtypeStruct((n, d), F32), jax.ShapeDtypeStruct((n, d), BF16)],
        compiler_params=_params(("arbitrary",)), name="outproj_post",
    )(*args)


def _ple_kernel(*refs, n_norm, n_parts):
    hp_ref, wg_ref, p_ref, wp_ref, x1_ref = refs[:5]
    g_refs = refs[5:5 + n_norm]
    x2_ref = refs[5 + n_norm]
    h_refs = refs[6 + n_norm:]
    rows_per = hp_ref.shape[0] // n_parts
    for c in range(n_parts):
        rows = slice(c * rows_per, (c + 1) * rows_per)
        gate = _sigmoid(jnp.dot(hp_ref[rows, :], wg_ref[...], preferred_element_type=F32))
        proj = jnp.dot(p_ref[rows, :].astype(BF16), wp_ref[...], preferred_element_type=F32)
        x2 = x1_ref[rows, :] + proj * gate
        x2_ref[rows, :] = x2
        if n_norm:
            xn = _rms_unit(x2)
            for g_ref, h_ref in zip(g_refs, h_refs):
                h_ref[rows, :] = (xn * g_ref[...]).astype(BF16)


def _ple(hp, w_gate, p, w_proj, x1, norm_gains):
    n, d = hp.shape
    pd = p.shape[1]
    tm = min(512, n)
    assert n % tm == 0
    n_norm = len(norm_gains)
    row = lambda width: pl.BlockSpec((tm, width), lambda i: (i, 0))
    in_specs = [row(d), _resident((d, d)), row(pd), _resident((pd, d)), row(d)] + \
               [_resident((1, d))] * n_norm
    out = pl.pallas_call(
        functools.partial(_ple_kernel, n_norm=n_norm, n_parts=2 if tm % 32 == 0 else 1),
        grid=(n // tm,),
        in_specs=in_specs, out_specs=[row(d)] * (1 + n_norm),
        out_shape=[jax.ShapeDtypeStruct((n, d), F32)] +
                  [jax.ShapeDtypeStruct((n, d), BF16)] * n_norm,
        compiler_params=_params(("arbitrary",)), name="ple",
    )(hp, w_gate, p, w_proj, x1, *norm_gains)
    return out[0], out[1:]


def _matmul_split_kernel(x_ref, w_ref, o0_ref, o1_ref, *, ns, f0, f1, n_parts):
    j = pl.program_id(1)
    ts = w_ref.shape[1] // n_parts

    def emit(o_ref, f):
        for c in range(n_parts):
            cols = slice(c * ts, (c + 1) * ts)
            acc = jnp.dot(x_ref[...], w_ref[:, cols], preferred_element_type=F32)
            o_ref[:, cols] = f(acc).astype(o_ref.dtype)

    @pl.when(j < ns)
    def _():
        emit(o0_ref, f0)

    @pl.when(j >= ns)
    def _():
        emit(o1_ref, f1)


def _matmul_split(x, w, dtypes, f0, f1, name):
    n, k = x.shape
    half = w.shape[1] // 2
    tm = min(1024, n)
    tn = min(1024, half)
    assert n % tm == 0 and half % tn == 0
    ns = half // tn
    return pl.pallas_call(
        functools.partial(_matmul_split_kernel, ns=ns, f0=f0, f1=f1,
                          n_parts=2 if tn % (2 * LANES) == 0 else 1),
        grid=(n // tm, 2 * ns),
        in_specs=[pl.BlockSpec((tm, k), lambda i, j: (i, 0)),
                  pl.BlockSpec((k, tn), lambda i, j: (0, j))],
        out_specs=[pl.BlockSpec((tm, tn), lambda i, j: (i, jnp.minimum(j, ns - 1))),
                   pl.BlockSpec((tm, tn), lambda i, j: (i, jnp.maximum(j - ns, 0)))],
        out_shape=[jax.ShapeDtypeStruct((n, half), dtypes[0]),
                   jax.ShapeDtypeStruct((n, half), dtypes[1])],
        compiler_params=_params(("arbitrary", "arbitrary")), name=name,
    )(x, w)


def _suffix_matrix(kb):
    assert kb & (kb - 1) == 0
    r = lax.broadcasted_iota(jnp.int32, (2 * kb, kb), 0) & (kb - 1)
    c = lax.broadcasted_iota(jnp.int32, (2 * kb, kb), 1)
    return jnp.where(r > c, 1.0, 0.0).astype(BF16)


def _add_rows(x, r0, delta):
    return x + delta if r0 == 0 else jnp.concatenate([x[:r0], x[r0:] + delta], axis=0)


def _sb_weights(z2s, masks, run, suffix, r0s):
    kb = z2s[0].shape[1]
    log_betas, neg_keeps, splits = [], [], []
    for z2, mask in zip(z2s, masks):
        sp = jnp.log2(1.0 + jnp.exp2(-jnp.abs(z2)))
        log_beta = jnp.minimum(z2, 0.0) - sp
        neg_keep = z2 - log_beta
        if mask is not None:
            neg_keep = jnp.where(mask, neg_keep, 0.0)
        hi = neg_keep.astype(BF16)
        lo = (neg_keep - hi.astype(F32)).astype(BF16)
        log_betas.append(log_beta)
        neg_keeps.append(neg_keep)
        splits.append(jnp.concatenate([hi, lo], axis=1))
    sums = [jnp.dot(sp2, suffix, preferred_element_type=F32) for sp2 in splits]
    weights = []
    for log_beta, neg_keep, cs, mask, r0 in zip(log_betas, neg_keeps, sums, masks, r0s):
        later = jnp.tile(run[r0:], (1, kb // LANES)) + cs
        a = jnp.exp2(log_beta - later)
        if mask is not None:
            a = jnp.where(mask, a, 0.0)
        weights.append(a)
        run = _add_rows(run, r0, cs[:, 0:1] + neg_keep[:, 0:1])
    return weights, run


_NT = (((1,), (1,)), ((), ()))


def _sb_prompt_kernel(bias_ref, q_ref, k_ref, v_ref, sg_ref, o_ref,
                      kb_ref, vb_ref, sfx_ref, run_ref, acc_ref, *, seq_len):
    h = pl.program_id(1)
    kb_ref[...] = k_ref[0].astype(BF16)
    vb_ref[...] = v_ref[0].astype(BF16)
    sfx_ref[...] = _suffix_matrix(PROMPT_KB)
    z_bias = bias_ref[h] * LOG2E

    def blocks(row0, key0s, r0s, masks, kb, suffix):
        q = q_ref[0, pl.ds(pl.multiple_of(row0, PROMPT_TQ), PROMPT_TQ), :]
        keys = [pl.ds(pl.multiple_of(key0, kb), kb) for key0 in key0s]
        z2s = [lax.dot_general(q[r0:], kb_ref[k, :], _NT, preferred_element_type=F32) + z_bias
               for k, r0 in zip(keys, r0s)]
        weights, run = _sb_weights(z2s, masks, run_ref[...], suffix, r0s)
        acc = acc_ref[...]
        for a, k, r0 in zip(weights, keys, r0s):
            acc = _add_rows(acc, r0, jnp.dot(a.astype(BF16), vb_ref[k, :],
                                             preferred_element_type=F32))
        run_ref[...] = run
        acc_ref[...] = acc

    def q_body(qi, _):
        row0 = qi * PROMPT_TQ
        run_ref[...] = jnp.zeros_like(run_ref)
        acc_ref[...] = jnp.zeros_like(acc_ref)
        dkb = PROMPT_DIAG_KB
        r0s = [c * dkb for c in reversed(range(PROMPT_TQ // dkb))]
        masks = [lax.broadcasted_iota(jnp.int32, (PROMPT_TQ - r0, dkb), 1) <
                 lax.broadcasted_iota(jnp.int32, (PROMPT_TQ - r0, dkb), 0) for r0 in r0s]
        diag_suffix = jnp.concatenate([sfx_ref[0:dkb, 0:dkb],
                                       sfx_ref[PROMPT_KB:PROMPT_KB + dkb, 0:dkb]], axis=0)
        blocks(row0, [row0 + r0 for r0 in r0s], r0s, masks, dkb, diag_suffix)
        n_full = qi * (PROMPT_TQ // PROMPT_KB)

        def k_body(t, _):
            first = n_full - 1 - t * PROMPT_UNROLL
            blocks(row0, [(first - c) * PROMPT_KB for c in range(PROMPT_UNROLL)],
                   [0] * PROMPT_UNROLL, [None] * PROMPT_UNROLL, PROMPT_KB, sfx_ref[...])
            return 0

        lax.fori_loop(0, n_full // PROMPT_UNROLL, k_body, 0)
        rows = pl.ds(pl.multiple_of(row0, PROMPT_TQ), PROMPT_TQ)
        o_ref[0, rows, :] = (acc_ref[...] * sg_ref[0, rows, :].astype(F32)).astype(o_ref.dtype)
        return 0

    lax.fori_loop(0, seq_len // PROMPT_TQ, q_body, 0)


def _sb_prompt(q, k, v, sgate, bias, n_heads):
    b, t, wb = q.shape
    assert t % PROMPT_TQ == 0 and wb == n_heads * HEAD_DIM
    assert (PROMPT_TQ // PROMPT_KB) % PROMPT_UNROLL == 0
    blk = pl.BlockSpec((1, t, HEAD_DIM), lambda bi, hi: (bi, 0, hi))
    return pl.pallas_call(
        functools.partial(_sb_prompt_kernel, seq_len=t), grid=(b, n_heads),
        in_specs=[pl.BlockSpec(memory_space=pltpu.SMEM), blk, blk, blk, blk],
        out_specs=blk,
        out_shape=jax.ShapeDtypeStruct((b, t, wb), BF16),
        scratch_shapes=[pltpu.VMEM((t, HEAD_DIM), BF16), pltpu.VMEM((t, HEAD_DIM), BF16),
                        pltpu.VMEM((2 * PROMPT_KB, PROMPT_KB), BF16),
                        pltpu.VMEM((PROMPT_TQ, LANES), F32),
                        pltpu.VMEM((PROMPT_TQ, HEAD_DIM), F32)],
        compiler_params=_params(("arbitrary", "arbitrary")), name="sb_prompt",
    )(bias, q, k, v, sgate)


def _sb_sample_kernel(pt_ref, q_ref, bias_ref, kn_ref, vn_ref, ck_ref, cv_ref, o_ref,
                      knew_ref, vnew_ref, sfx_ref, run_ref, acc_ref, kbuf_ref, vbuf_ref, sem_ref,
                      *, n_heads, t_new, pps, n_pages):
    n_oct = n_heads // HEAD_OCTET
    per_slot = pps * n_oct
    s = pl.program_id(0)
    p = pl.program_id(1)
    rows = n_heads * Q_PAD
    pages_per_block = SAMPLE_KB // KEY_BLOCK
    n_steps = n_pages // pps
    total_steps = pl.num_programs(0) * n_steps
    step = s * n_steps + p

    def page_copies(g, slot):
        gs, gp = g // n_steps, g % n_steps
        copies = []
        for c in range(pps):
            page = pt_ref[gs, n_pages - pps * (gp + 1) + c]
            for o in range(n_oct):
                heads = pl.ds(o * HEAD_OCTET, HEAD_OCTET)
                for cache, buf in ((ck_ref, kbuf_ref), (cv_ref, vbuf_ref)):
                    copies.append(pltpu.make_async_copy(
                        cache.at[page, :, heads, :], buf.at[slot * per_slot + c * n_oct + o],
                        sem_ref.at[slot]))
        return copies

    def start_pages(g):
        for cp in page_copies(g, g % SAMPLE_RING):
            cp.start()

    @pl.when(step == 0)
    def _():
        for g in range(SAMPLE_RING - 1):
            start_pages(g)

    @pl.when(step + SAMPLE_RING - 1 < total_steps)
    def _():
        start_pages(step + SAMPLE_RING - 1)

    def blocks(k_blocks, v_blocks, mask, suffix):
        kb = k_blocks[0][0].shape[0]
        z_bias = jnp.tile(bias_ref[...] * LOG2E, (1, kb // LANES))
        z2s = [jnp.concatenate(
            [lax.dot_general(q_ref[0, h * Q_PAD:(h + 1) * Q_PAD, :], k_heads[h], _NT,
                             preferred_element_type=F32) for h in range(n_heads)], axis=0) + z_bias
               for k_heads in k_blocks]
        nb = len(z2s)
        weights, run = _sb_weights(z2s, [mask] * nb, run_ref[...], suffix, [0] * nb)
        run_ref[...] = run
        a = jnp.concatenate([w.astype(BF16) for w in reversed(weights)], axis=1)
        for h in range(n_heads):
            r = slice(h * Q_PAD, (h + 1) * Q_PAD)
            v = jnp.concatenate([v_heads[h] for v_heads in reversed(v_blocks)], axis=0)
            acc_ref[r, :] += jnp.dot(a[r, :], v, preferred_element_type=F32)

    def page_heads(buf_ref, slot, block):
        out = []
        for h in range(n_heads):
            parts = []
            for c in range(block * pages_per_block, (block + 1) * pages_per_block):
                octet = buf_ref.at[slot * per_slot + c * n_oct + h // HEAD_OCTET]
                flat = octet.reshape(KEY_BLOCK * HEAD_OCTET, HEAD_DIM)
                parts.append(flat[pl.ds(h % HEAD_OCTET, KEY_BLOCK, stride=HEAD_OCTET), :])
            out.append(jnp.concatenate(parts, axis=0).astype(BF16))
        return out

    @pl.when((s == 0) & (p == 0))
    def _():
        knew_ref[...] = jnp.zeros_like(knew_ref)
        vnew_ref[...] = jnp.zeros_like(vnew_ref)
        sfx_ref[...] = _suffix_matrix(SAMPLE_KB)

    @pl.when(p == 0)
    def _():
        knew_ref[0:t_new, :] = kn_ref[0]
        vnew_ref[0:t_new, :] = vn_ref[0]
        run_ref[...] = jnp.zeros_like(run_ref)
        acc_ref[...] = jnp.zeros_like(acc_ref)
        qpos = lax.broadcasted_iota(jnp.int32, (rows, KEY_BLOCK), 0) % Q_PAD
        kpos = lax.broadcasted_iota(jnp.int32, (rows, KEY_BLOCK), 1)
        lanes = lambda ref: [ref[:, h * HEAD_DIM:(h + 1) * HEAD_DIM].astype(BF16)
                             for h in range(n_heads)]
        blocks([lanes(knew_ref)], [lanes(vnew_ref)], kpos < qpos, _suffix_matrix(KEY_BLOCK))

    order = list(reversed(range(pps // pages_per_block)))
    for slot in range(SAMPLE_RING):
        @pl.when(step % SAMPLE_RING == slot)
        def _(slot=slot):
            for cp in page_copies(step, slot):
                cp.wait()
            blocks([page_heads(kbuf_ref, slot, b) for b in order],
                   [page_heads(vbuf_ref, slot, b) for b in order], None, sfx_ref[...])

    @pl.when(p == pl.num_programs(1) - 1)
    def _():
        o_ref[0] = acc_ref[...]


def _sb_sample(q, bias_rows, k_new, v_new, cache_k, cache_v, page_table, n_heads):
    s, t_new, wb = k_new.shape
    n_pages = page_table.shape[1]
    pps = SAMPLE_PAGES_PER_STEP
    assert cache_k.shape[1:] == (KEY_BLOCK, n_heads, HEAD_DIM) and t_new < Q_PAD
    assert n_pages % pps == 0 and n_heads % HEAD_OCTET == 0
    n_oct = n_heads // HEAD_OCTET
    rows = n_heads * Q_PAD
    assert (pps * KEY_BLOCK) % SAMPLE_KB == 0 and SAMPLE_KB % KEY_BLOCK == 0
    assert s * (n_pages // pps) >= SAMPLE_RING - 1

    new = pl.BlockSpec((1, t_new, wb), lambda si, pi, pt: (si, 0, 0))
    qo = pl.BlockSpec((1, rows, HEAD_DIM), lambda si, pi, pt: (si, 0, 0))
    bias_spec = pl.BlockSpec((rows, LANES), lambda si, pi, pt: (0, 0))
    hbm = pl.BlockSpec(memory_space=pl.ANY)
    ring = pltpu.VMEM((SAMPLE_RING * pps * n_oct, KEY_BLOCK, HEAD_OCTET, HEAD_DIM), F32)
    grid_spec = pltpu.PrefetchScalarGridSpec(
        num_scalar_prefetch=1, grid=(s, n_pages // pps),
        in_specs=[qo, bias_spec, new, new, hbm, hbm],
        out_specs=qo,
        scratch_shapes=[pltpu.VMEM((KEY_BLOCK, wb), F32), pltpu.VMEM((KEY_BLOCK, wb), F32),
                        pltpu.VMEM((2 * SAMPLE_KB, SAMPLE_KB), BF16),
                        pltpu.VMEM((rows, LANES), F32), pltpu.VMEM((rows, HEAD_DIM), F32),
                        ring, ring, pltpu.SemaphoreType.DMA((SAMPLE_RING,))])
    return pl.pallas_call(
        functools.partial(_sb_sample_kernel, n_heads=n_heads, t_new=t_new, pps=pps,
                          n_pages=n_pages),
        grid_spec=grid_spec,
        out_shape=jax.ShapeDtypeStruct((s, rows, HEAD_DIM), F32),
        compiler_params=_params(("arbitrary", "arbitrary")), name="sb_sample",
    )(page_table, q, bias_rows, k_new, v_new, cache_k, cache_v)


def _trunk(x, p, weights, past):
    (pre_g, post_g, a_w_in, a_w_group, a_scale, a_w_out, kv_g, w_kv, b_w_in, b_bias, b_w_out,
     ple_g, ple_w_gate, ple_w_proj) = weights
    b, t, d = x.shape
    n = b * t
    n_heads = b_bias.shape[1]
    wa = a_w_out.shape[1]
    xf = x.reshape(n, d)
    pf = p.reshape(p.shape[0], n, p.shape[-1])
    gain = lambda g: g.reshape(1, -1)

    if past is None:
        pooled, sgate, tails = _inproj(xf, gain(pre_g[0]), a_w_in[0], seq_len=t, fuse_pool=True)
        tails = tails.reshape(b, -1, POOL_CARRY, wa)[:, -1]
        pool_state = tails[None, :, POOL_CARRY - max(POOL_WINDOWS) + 1:, :]
    else:
        u, sgate = _inproj(xf, gain(pre_g[0]), a_w_in[0], seq_len=t, fuse_pool=False)
        pooled, pool_state = _pool_sample(past["pool"], u.reshape(b, t, wa))
        pooled = pooled.reshape(n, wa)
    y = _group_mix(pooled, a_w_group[0], gain(a_scale[0]), sgate)
    x1, hp = _outproj(y, None, a_w_out[0], xf, gain(post_g[0]), gain(ple_g[0]))
    x2, (hk, hb) = _ple(hp, ple_w_gate[0], pf[0], ple_w_proj[0], x1, [gain(kv_g), gain(pre_g[1])])

    ident = lambda a: a
    k_new, v_new = _matmul_split(hk, w_kv, (F32, F32), ident, ident, "kv_proj")
    q_scale = HEAD_DIM ** -0.5 * LOG2E
    q, sgate_b = _matmul_split(hb, b_w_in[0], (BF16, BF16), lambda a: a * q_scale, _silu,
                               "q_proj")

    if past is None:
        shp = (b, t, n_heads * HEAD_DIM)
        y2 = _sb_prompt(q.reshape(shp), k_new.reshape(shp), v_new.reshape(shp),
                        sgate_b.reshape(shp), b_bias[0], n_heads).reshape(n, -1)
        x3, hp = _outproj(y2, None, b_w_out[0], x2, gain(post_g[1]), gain(ple_g[1]))
    else:
        q4 = q.reshape(b, t, n_heads, HEAD_DIM).transpose(0, 2, 1, 3)
        q4 = jnp.pad(q4, ((0, 0), (0, 0), (0, Q_PAD - t), (0, 0)))
        bias_rows = jnp.broadcast_to(jnp.repeat(b_bias[0], Q_PAD)[:, None],
                                     (n_heads * Q_PAD, LANES))
        o = _sb_sample(q4.reshape(b, n_heads * Q_PAD, HEAD_DIM), bias_rows,
                       k_new.reshape(b, t, -1), v_new.reshape(b, t, -1),
                       past["cache_k"], past["cache_v"], past["page_table"], n_heads)
        o = o.reshape(b, n_heads, Q_PAD, HEAD_DIM)[:, :, :t].transpose(0, 2, 1, 3).reshape(n, -1)
        x3, hp = _outproj(o, sgate_b, b_w_out[0], x2, gain(post_g[1]), gain(ple_g[1]))
    x4, _ = _ple(hp, ple_w_gate[1], pf[1], ple_w_proj[1], x3, [])

    kv_shape = (b, t, n_heads, HEAD_DIM)
    return x4.reshape(b, t, d), k_new.reshape(kv_shape), v_new.reshape(kv_shape), pool_state


def kernel(x_prompt, x_sample, cache_k, cache_v, state_pool, page_table, p_prompt, p_sample,
           pre_norm_g, post_norm_g, a_w_in, a_w_group, a_scale, a_w_out, kv_norm_g, w_kv,
           b_w_in, b_logit_bias, b_w_out, ple_norm_g, ple_w_gate, ple_w_proj):
    assert pre_norm_g.shape[0] == 2 and a_w_in.shape[0] == 1 and b_w_in.shape[0] == 1
    bf = lambda a: a.astype(BF16)
    weights = (pre_norm_g, post_norm_g, bf(a_w_in), bf(a_w_group), a_scale, bf(a_w_out),
               kv_norm_g, bf(w_kv), bf(b_w_in), b_logit_bias, bf(b_w_out), ple_norm_g,
               bf(ple_w_gate), bf(ple_w_proj))
    past = {"pool": state_pool, "cache_k": cache_k, "cache_v": cache_v,
            "page_table": page_table}
    y_p, k_p, v_p, st_p = _trunk(x_prompt, p_prompt, weights, None)
    y_s, k_s, v_s, st_s = _trunk(x_sample, p_sample, weights, past)
    return (y_p, y_s, k_p, v_p, k_s, v_s, st_p, st_s)
```
